```python
import math
import jax
import jax.numpy as jnp
from jax import lax
import numpy as np

D_MODEL = 1024
BATCH = 1
SEQ = 16384
DEPTH = 1

HEAD_DIM = 64
FOX_HEADS = 6
DIL_CONFIGS = ((128, 1), (512, 4), (2048, 16))
DIL_HEADS_PER_GROUP = 2
DIL_GROUPS = len(DIL_CONFIGS)
DIL_HEADS = DIL_HEADS_PER_GROUP * DIL_GROUPS
MEM_HEADS = 4
MEM_LEN = 256
FOX_W = FOX_HEADS * HEAD_DIM
DIL_W = DIL_HEADS * HEAD_DIM
MEM_W = MEM_HEADS * HEAD_DIM
N_BRANCHES = 3
QBLK = 128
DBLK = 128
T5_BUCKETS = 32
T5_MAX_EXACT = T5_BUCKETS // 2
T5_MAX_DISTANCE = 2048
N_EXPERTS = 32
TOP_K = 4
D_FF_EXPERT = D_MODEL
SWIGLU_LIMIT = 7.0
SWIGLU_ALPHA = 1.702
EBLK = 128
LN_EPS = 1e-5
DEEPNORM_ALPHA = (2 * DEPTH) ** 0.25
DEEPNORM_BETA = (8 * DEPTH) ** -0.25

OFF_FOX_QKV = 0
OFF_FOX_F = OFF_FOX_QKV + 3 * FOX_W
OFF_DIL_QKV = OFF_FOX_F + FOX_HEADS
OFF_MEM_Q = OFF_DIL_QKV + 3 * DIL_W
OFF_GATES = OFF_MEM_Q + MEM_W
D_IN = OFF_GATES + N_BRANCHES * D_MODEL

kernel_name = 'hybrid_fox_dilated_mem_moe_block'


def layer_norm(x, g, b):
    xf = x.astype(jnp.float32)
    mu = jnp.mean(xf, axis=-1, keepdims=True)
    var = jnp.mean(jnp.square(xf - mu), axis=-1, keepdims=True)
    y = (xf - mu) * lax.rsqrt(var + LN_EPS) * g.astype(jnp.float32) + b.astype(jnp.float32)
    return y.astype(x.dtype)


def t5_bucket(dist):
    is_small = dist < T5_MAX_EXACT
    nf = jnp.maximum(dist, T5_MAX_EXACT).astype(jnp.float32)
    large = T5_MAX_EXACT + (jnp.log(nf / T5_MAX_EXACT) / math.log(T5_MAX_DISTANCE / T5_MAX_EXACT)
                            * (T5_BUCKETS - T5_MAX_EXACT)).astype(jnp.int32)
    large = jnp.minimum(large, T5_BUCKETS - 1)
    return jnp.where(is_small, dist, large)


def fox_attention(q, k, v, log_f):
    B, S, H, Dh = q.shape
    nblk = S // QBLK
    scale = Dh ** -0.5
    c = jnp.cumsum(log_f, axis=1)
    c_keys = c.transpose(0, 2, 1)
    qb = jnp.moveaxis(q.reshape(B, nblk, QBLK, H, Dh), 1, 0)
    cb = jnp.moveaxis(c_keys.reshape(B, H, nblk, QBLK), 2, 0)
    key_pos = jnp.arange(S)

    def block(args):
        n, qn, cn = args
        s = jnp.einsum('bqhd,bkhd->bhqk', qn, k, preferred_element_type=jnp.float32) * scale
        s = s + (cn[..., :, None] - c_keys[:, :, None, :])
        q_pos = n * QBLK + jnp.arange(QBLK)
        causal = key_pos[None, :] <= q_pos[:, None]
        s = jnp.where(causal, s, -jnp.inf)
        p = jax.nn.softmax(s, axis=-1)
        return jnp.einsum('bhqk,bkhd->bqhd', p.astype(v.dtype), v)

    out = lax.map(block, (jnp.arange(nblk), qb, cb))
    return jnp.moveaxis(out, 0, 1).reshape(B, S, H, Dh)


def dilated_group(q, k, v, table, window, dil):
    B, S, H, Dh = q.shape
    n_back = window // dil
    unit = dil * DBLK
    Sp = -(-S // unit) * unit
    M = Sp // dil
    nb = M // DBLK
    scale = Dh ** -0.5

    def to_blocks(a):
        a = jnp.pad(a, ((0, 0), (0, Sp - S), (0, 0), (0, 0)))
        a = a.reshape(B, M, dil, H, Dh).transpose(0, 2, 1, 3, 4)
        return a.reshape(B, dil, nb, DBLK, H, Dh)

    def with_prev(a):
        prev = jnp.pad(a, ((0, 0), (0, 0), (1, 0), (0, 0), (0, 0), (0, 0)))[:, :, :-1]
        return jnp.concatenate([prev, a], axis=3)

    qb = to_blocks(q)
    kc = with_prev(to_blocks(k))
    vc = with_prev(to_blocks(v))
    s = jnp.einsum('brnqhd,brnkhd->brnhqk', qb, kc, preferred_element_type=jnp.float32) * scale

    i = np.arange(DBLK, dtype=np.int32)[:, None]
    j = np.arange(2 * DBLK, dtype=np.int32)[None, :]
    rel = DBLK + i - j
    in_window = (rel >= 0) & (rel <= n_back)
    bucket = t5_bucket(jnp.asarray(np.clip(rel, 0, None) * dil, dtype=jnp.int32))
    bias = table[bucket].astype(jnp.float32).transpose(2, 0, 1)
    s = s + bias
    has_prev = np.arange(nb)[:, None, None] > 0
    mask = in_window[None] & (has_prev | (j >= DBLK)[None])
    s = jnp.where(mask[None, None, :, None], s, -jnp.inf)

    m = jnp.max(s, axis=-1, keepdims=True)
    e = jnp.exp(s - m)
    den = jnp.sum(e, axis=-1, keepdims=True)
    p = e / den
    lse = (m + jnp.log(den))[..., 0]
    o = jnp.einsum('brnhqk,brnkhd->brnqhd', p.astype(vc.dtype), vc)
    o = o.reshape(B, dil, M, H, Dh).transpose(0, 2, 1, 3, 4).reshape(B, Sp, H, Dh)[:, :S]
    lse = lse.transpose(0, 1, 2, 4, 3).reshape(B, dil, M, H).transpose(0, 2, 1, 3).reshape(B, Sp, H)[:, :S]
    return o, lse


def mem_attention(q, mem, w_kv):
    B, S, H, Dh = q.shape
    M = mem.shape[1]
    kv = jnp.einsum('bmd,de->bme', mem, w_kv).reshape(B, M, 2, H, Dh)
    k, v = kv[:, :, 0], kv[:, :, 1]
    s = jnp.einsum('bshd,bmhd->bhsm', q, k, preferred_element_type=jnp.float32) * (Dh ** -0.5)
    p = jax.nn.softmax(s, axis=-1)
    return jnp.einsum('bhsm,bmhd->bshd', p.astype(v.dtype), v)


def clamped_swiglu(h):
    h_glu, h_lin = jnp.split(h, 2, axis=-1)
    h_glu = jnp.minimum(h_glu, SWIGLU_LIMIT)
    h_lin = jnp.clip(h_lin, -SWIGLU_LIMIT, SWIGLU_LIMIT)
    return h_glu * jax.nn.sigmoid(SWIGLU_ALPHA * h_glu) * (h_lin + 1.0)


def moe_ffn(xt, w_router, b_router, w1, b1, w2, b2):
    T, D = xt.shape
    A = T * TOP_K
    logits = jnp.einsum('td,de->te', xt, w_router, preferred_element_type=jnp.float32) + b_router.astype(jnp.float32)
    top_vals, top_idx = lax.top_k(logits, TOP_K)
    gates = jax.nn.softmax(top_vals, axis=-1)
    flat_e = top_idx.reshape(A)
    flat_tok = jnp.broadcast_to(jnp.arange(T, dtype=jnp.int32)[:, None], (T, TOP_K)).reshape(A)
    flat_g = gates.reshape(A)
    order = jnp.argsort(flat_e)
    sorted_e = flat_e[order]
    counts = jnp.bincount(flat_e, length=N_EXPERTS)
    starts = jnp.cumsum(counts) - counts
    padded = (counts + EBLK - 1) // EBLK * EBLK
    pad_ends = jnp.cumsum(padded)
    pad_starts = pad_ends - padded
    dest = pad_starts[sorted_e] + (jnp.arange(A) - starts[sorted_e])
    n_blocks = -(-(A + N_EXPERTS * EBLK) // EBLK)
    P = n_blocks * EBLK
    tok_buf = jnp.full((P,), T, dtype=jnp.int32).at[dest].set(flat_tok[order])
    gate_buf = jnp.zeros((P,), jnp.float32).at[dest].set(flat_g[order])
    blk_e = jnp.minimum(jnp.searchsorted(pad_ends, jnp.arange(n_blocks) * EBLK, side='right'), N_EXPERTS - 1)
    x_pad = jnp.concatenate([xt, jnp.zeros((1, D), xt.dtype)], axis=0)
    xb = x_pad[tok_buf].reshape(n_blocks, EBLK, D)

    def expert_block(args):
        xe, e = args
        h = xe @ w1[e] + b1[e]
        return clamped_swiglu(h) @ w2[e] + b2[e]

    yb = lax.map(expert_block, (xb, blk_e)).reshape(P, D)
    out = jnp.zeros((T + 1, D), jnp.float32).at[tok_buf].add(yb.astype(jnp.float32) * gate_buf[:, None])
    return out[:T].astype(xt.dtype)


def setup_inputs(seed: int = 0) -> dict:
    key = jax.random.key(seed)
    ks = jax.random.split(key, 32)

    def nrm(k, shape, scale):
        return jax.random.normal(k, shape, jnp.float32) * scale

    sd = D_MODEL ** -0.5
    beta = DEEPNORM_BETA
    x = nrm(ks[0], (BATCH, SEQ, D_MODEL), 1.0)
    mem = nrm(ks[1], (BATCH, MEM_LEN, D_MODEL), 1.0)
    w_in = jnp.concatenate([
        nrm(ks[2], (DEPTH, D_MODEL, 2 * FOX_W), sd),
        nrm(ks[3], (DEPTH, D_MODEL, FOX_W), sd * beta),
        nrm(ks[4], (DEPTH, D_MODEL, FOX_HEADS), 0.5 * sd),
        nrm(ks[5], (DEPTH, D_MODEL, 2 * DIL_W), sd),
        nrm(ks[6], (DEPTH, D_MODEL, DIL_W), sd * beta),
        nrm(ks[7], (DEPTH, D_MODEL, MEM_W), sd),
        nrm(ks[8], (DEPTH, D_MODEL, N_BRANCHES * D_MODEL), sd),
    ], axis=-1)
    b_fgate = jnp.linspace(1.0, 6.0, FOX_HEADS, dtype=jnp.float32)[None, :] + nrm(ks[9], (DEPTH, FOX_HEADS), 0.1)
    t5_bias = nrm(ks[10], (T5_BUCKETS, DIL_HEADS), 0.5)
    w_mem_kv = jnp.concatenate([
        nrm(ks[11], (DEPTH, D_MODEL, MEM_W), sd),
        nrm(ks[12], (DEPTH, D_MODEL, MEM_W), sd * beta),
    ], axis=-1)
    w_br_fox = nrm(ks[13], (DEPTH, FOX_W, D_MODEL), FOX_W ** -0.5 * beta)
    w_br_dil = nrm(ks[14], (DEPTH, DIL_W, D_MODEL), DIL_W ** -0.5 * beta)
    w_br_mem = nrm(ks[15], (DEPTH, MEM_W, D_MODEL), MEM_W ** -0.5 * beta)
    w_out = nrm(ks[16], (DEPTH, D_MODEL, D_MODEL), sd * beta)
    ln1_g = 1.0 + nrm(ks[17], (DEPTH, D_MODEL), 0.02)
    ln1_b = nrm(ks[18], (DEPTH, D_MODEL), 0.02)
    w_router = nrm(ks[19], (DEPTH, D_MODEL, N_EXPERTS), sd)
    b_router = nrm(ks[20], (DEPTH, N_EXPERTS), 0.01)
    w_exp_in = nrm(ks[21], (DEPTH, N_EXPERTS, D_MODEL, 2 * D_FF_EXPERT), sd * beta)
    b_exp_in = nrm(ks[22], (DEPTH, N_EXPERTS, 2 * D_FF_EXPERT), 0.01)
    w_exp_out = nrm(ks[23], (DEPTH, N_EXPERTS, D_FF_EXPERT, D_MODEL), D_FF_EXPERT ** -0.5 * beta)
    b_exp_out = nrm(ks[24], (DEPTH, N_EXPERTS, D_MODEL), 0.01)
    ln2_g = 1.0 + nrm(ks[25], (DEPTH, D_MODEL), 0.02)
    ln2_b = nrm(ks[26], (DEPTH, D_MODEL), 0.02)
    return {'x': x, 'mem': mem, 'w_in': w_in, 'b_fgate': b_fgate, 't5_bias': t5_bias,
            'w_mem_kv': w_mem_kv, 'w_br_fox': w_br_fox, 'w_br_dil': w_br_dil, 'w_br_mem': w_br_mem,
            'w_out': w_out, 'ln1_g': ln1_g, 'ln1_b': ln1_b, 'w_router': w_router, 'b_router': b_router,
            'w_exp_in': w_exp_in, 'b_exp_in': b_exp_in, 'w_exp_out': w_exp_out, 'b_exp_out': b_exp_out,
            'ln2_g': ln2_g, 'ln2_b': ln2_b}


def reference(x, mem, w_in, b_fgate, t5_bias, w_mem_kv, w_br_fox, w_br_dil, w_br_mem, w_out,
              ln1_g, ln1_b, w_router, b_router, w_exp_in, b_exp_in, w_exp_out, b_exp_out, ln2_g, ln2_b):
    B, S, D = x.shape
    h = x
    for l in range(DEPTH):
        proj = jnp.einsum('bsd,de->bse', h, w_in[l])
        fox_qkv = proj[..., OFF_FOX_QKV:OFF_FOX_F].reshape(B, S, 3, FOX_HEADS, HEAD_DIM)
        f_logit = proj[..., OFF_FOX_F:OFF_DIL_QKV] + b_fgate[l]
        log_f = jax.nn.log_sigmoid(f_logit.astype(jnp.float32))
        o_fox = fox_attention(fox_qkv[:, :, 0], fox_qkv[:, :, 1], fox_qkv[:, :, 2], log_f)
        o_fox = o_fox.reshape(B, S, FOX_W)

        dil_qkv = proj[..., OFF_DIL_QKV:OFF_MEM_Q].reshape(B, S, 3, DIL_HEADS, HEAD_DIM)
        outs = []
        lses = []
        for g, (window, dil) in enumerate(DIL_CONFIGS):
            hs = slice(g * DIL_HEADS_PER_GROUP, (g + 1) * DIL_HEADS_PER_GROUP)
            o_g, lse_g = dilated_group(dil_qkv[:, :, 0, hs], dil_qkv[:, :, 1, hs], dil_qkv[:, :, 2, hs],
                                       t5_bias[:, hs], window, dil)
            outs.append(o_g)
            lses.append(lse_g)
        o_stack = jnp.stack(outs, axis=2)
        w_den = jax.nn.softmax(jnp.stack(lses, axis=2), axis=2)
        o_dil = (o_stack * w_den[..., None].astype(o_stack.dtype)).reshape(B, S, DIL_W)

        mem_q = proj[..., OFF_MEM_Q:OFF_GATES].reshape(B, S, MEM_HEADS, HEAD_DIM)
        o_mem = mem_attention(mem_q, mem, w_mem_kv[l]).reshape(B, S, MEM_W)

        gates = jax.nn.sigmoid(proj[..., OFF_GATES:].reshape(B, S, N_BRANCHES, D))
        merged = (gates[:, :, 0] * jnp.einsum('bse,ed->bsd', o_fox, w_br_fox[l])
                  + gates[:, :, 1] * jnp.einsum('bse,ed->bsd', o_dil, w_br_dil[l])
                  + gates[:, :, 2] * jnp.einsum('bse,ed->bsd', o_mem, w_br_mem[l]))
        y = jnp.einsum('bsd,de->bse', merged, w_out[l])
        h = layer_norm(DEEPNORM_ALPHA * h + y, ln1_g[l], ln1_b[l])

        y_moe = moe_ffn(h.reshape(B * S, D), w_router[l], b_router[l], w_exp_in[l], b_exp_in[l],
                        w_exp_out[l], b_exp_out[l]).reshape(B, S, D)
        h = layer_norm(DEEPNORM_ALPHA * h + y_moe, ln2_g[l], ln2_b[l])
    return h
```

```python
import functools

import numpy as np
import jax
import jax.numpy as jnp
from jax import lax
from jax.experimental import pallas as pl
from jax.experimental.pallas import tpu as pltpu

D_MODEL = 1024
HEAD_DIM = 64
LANES = 128
FOX_HEADS = 6
FOX_PAIRS = FOX_HEADS // 2
DIL_CONFIGS = ((128, 1), (512, 4), (2048, 16))
DIL_GROUPS = len(DIL_CONFIGS)
DIL_HEADS = 2 * DIL_GROUPS
MEM_HEADS = 4
MEM_PAIRS = MEM_HEADS // 2
MEM_LEN = 256
FOX_W = FOX_HEADS * HEAD_DIM
DIL_W = DIL_HEADS * HEAD_DIM
MEM_W = MEM_HEADS * HEAD_DIM
DBLK = 128
T5_BUCKETS = 32
T5_MAX_EXACT = T5_BUCKETS // 2
T5_MAX_DISTANCE = 2048
N_EXPERTS = 32
TOP_K = 4
D_FF = D_MODEL
SWIGLU_LIMIT = 7.0
SWIGLU_ALPHA = 1.702
LN_EPS = 1e-5
DEEPNORM_ALPHA = 2.0 ** 0.25
QK_SCALE = HEAD_DIM ** -0.5

OFF_FOX_QKV = 0
OFF_FOX_F = OFF_FOX_QKV + 3 * FOX_W
OFF_DIL_QKV = OFF_FOX_F + FOX_HEADS
OFF_MEM_Q = OFF_DIL_QKV + 3 * DIL_W
OFF_GATES = OFF_MEM_Q + MEM_W

QKV_FOX = 0
QKV_DIL = 3 * FOX_W
QKV_MEM = QKV_DIL + 3 * DIL_W
QKV_W = QKV_MEM + MEM_W

VMEM_LIMIT = 56 * 1024 * 1024

PROJ_TM = 512
FOX_TQ = 256
FOX_TK = 256
MERGE_TM = 256
DISPATCH_TM = 256
EXPERT_TM = 256
COMBINE_TM = 256

BF16 = jnp.bfloat16
F32 = jnp.float32


def _dot(a, b):
    return jnp.dot(a, b, preferred_element_type=F32)


def _dot_nt(a, b):
    return lax.dot_general(a, b, (((1,), (1,)), ((), ())), preferred_element_type=F32)


def _lane_lt(shape, n):
    return lax.broadcasted_iota(jnp.int32, shape, len(shape) - 1) < n


def _split3(v):
    hi = v.astype(BF16)
    r1 = v - hi.astype(F32)
    mid = r1.astype(BF16)
    lo = (r1 - mid.astype(F32)).astype(BF16)
    return hi, mid, lo


def _layer_norm(r, g, b):
    mu = jnp.mean(r, axis=-1, keepdims=True)
    d = r - mu
    var = jnp.mean(d * d, axis=-1, keepdims=True)
    return d * lax.rsqrt(var + LN_EPS) * g + b


def _proj_kernel(x_ref, w_ref, wft_ref, bf_ref, mem_ref, wkv_ref, qkv_ref, c_ref, omem_ref,
                 kvm_ref, carry_ref):
    i = pl.program_id(0)
    tm = x_ref.shape[0]

    @pl.when(i == 0)
    def _():
        kvm_ref[...] = _dot(mem_ref[...], wkv_ref[...]).astype(BF16)
        carry_ref[...] = jnp.zeros_like(carry_ref)

    xb = x_ref[...].astype(BF16)
    proj = _dot(xb, w_ref[...])

    def put(lo, hi, scale):
        v = proj[:, lo:hi]
        if scale is not None:
            v = v * scale
        qkv_ref[:, lo:hi] = v.astype(BF16)

    put(QKV_FOX, QKV_FOX + FOX_W, QK_SCALE)
    put(QKV_FOX + FOX_W, QKV_DIL, None)
    put(QKV_DIL, QKV_DIL + DIL_W, QK_SCALE)
    put(QKV_DIL + DIL_W, QKV_MEM, None)

    z = _dot_nt(wft_ref[...], xb) + bf_ref[...]
    logf = jnp.minimum(z, 0.0) - jnp.log1p(jnp.exp(-jnp.abs(z)))
    row = lax.broadcasted_iota(jnp.int32, (tm, tm), 0)
    col = lax.broadcasted_iota(jnp.int32, (tm, tm), 1)
    upper = jnp.where(row <= col, 1.0, 0.0).astype(BF16)
    hi, mid, lo = _split3(logf)
    csum = (_dot(lo, upper) + _dot(mid, upper)) + _dot(hi, upper)
    c_new = csum + carry_ref[:, 0:1]
    c_ref[...] = c_new
    carry_ref[...] = jnp.broadcast_to(c_new[:, tm - 1:tm], carry_ref.shape)

    qm = (proj[:, QKV_MEM:QKV_W] * QK_SCALE).astype(BF16)
    for p in range(MEM_PAIRS):
        qp = qm[:, p * LANES:(p + 1) * LANES]
        kp = kvm_ref[:, p * LANES:(p + 1) * LANES]
        vp = kvm_ref[:, MEM_W + p * LANES:MEM_W + (p + 1) * LANES]
        first = _lane_lt(qp.shape, HEAD_DIM)
        outs = []
        for a in range(2):
            qa = jnp.where(first if a == 0 else jnp.logical_not(first), qp, jnp.zeros_like(qp))
            s = _dot_nt(qa, kp)
            m = jnp.max(s, axis=-1, keepdims=True)
            e = jnp.exp(s - m)
            den = jnp.sum(e, axis=-1, keepdims=True)
            outs.append(_dot(e.astype(BF16), vp) / den)
        omem_ref[:, p * LANES:(p + 1) * LANES] = jnp.where(first, outs[0], outs[1]).astype(BF16)


def _proj(x2, w_qkv, wft, bf, mem_b, wkv_b):
    s = x2.shape[0]
    tm = min(PROJ_TM, s)
    full = lambda shape: pl.BlockSpec(shape, lambda i: (0,) * len(shape))
    return pl.pallas_call(
        _proj_kernel,
        grid=(s // tm,),
        in_specs=[pl.BlockSpec((tm, D_MODEL), lambda i: (i, 0)),
                  full(w_qkv.shape), full(wft.shape), full(bf.shape), full(mem_b.shape), full(wkv_b.shape)],
        out_specs=[pl.BlockSpec((tm, QKV_MEM), lambda i: (i, 0)),
                   pl.BlockSpec((8, tm), lambda i: (0, i)),
                   pl.BlockSpec((tm, MEM_W), lambda i: (i, 0))],
        out_shape=[jax.ShapeDtypeStruct((s, QKV_MEM), BF16),
                   jax.ShapeDtypeStruct((8, s), F32),
                   jax.ShapeDtypeStruct((s, MEM_W), BF16)],
        scratch_shapes=[pltpu.VMEM((MEM_LEN, 2 * MEM_W), BF16), pltpu.VMEM((8, LANES), F32)],
        compiler_params=pltpu.CompilerParams(dimension_semantics=("arbitrary",),
                                             vmem_limit_bytes=VMEM_LIMIT),
        name="proj",
    )(x2, w_qkv, wft, bf, mem_b, wkv_b)


def _fox_kernel(q_ref, k_ref, v_ref, c_ref, cref_ref, o_ref):
    i = pl.program_id(1)
    tq = q_ref.shape[0]
    tk = FOX_TK
    q = q_ref[...]
    first = _lane_lt(q.shape, HEAD_DIM)
    qa = (jnp.where(first, q, jnp.zeros_like(q)), jnp.where(first, jnp.zeros_like(q), q))

    def step(start, carry, masked_from):
        kb = k_ref[pl.ds(start, tk), :]
        vb = v_ref[pl.ds(start, tk), :]
        cb = c_ref[:, pl.ds(start, tk)]
        new = []
        for a in range(2):
            m_prev, l_prev, acc_prev = carry[a]
            bias = cref_ref[a:a + 1, :] [:, 0:1] - cb[a:a + 1, :]
            s = _dot_nt(qa[a], kb) + bias
            if masked_from is not None:
                qpos = lax.broadcasted_iota(jnp.int32, (tq, tk), 0)
                kpos = lax.broadcasted_iota(jnp.int32, (tq, tk), 1) + masked_from
                s = jnp.where(kpos <= qpos, s, -jnp.inf)
            m_new = jnp.maximum(m_prev, jnp.max(s, axis=-1, keepdims=True))
            alpha = jnp.exp(m_prev - m_new)
            p = jnp.exp(s - m_new)
            l_new = alpha * l_prev + jnp.sum(p, axis=-1, keepdims=True)
            acc_new = alpha * acc_prev + _dot(p.astype(BF16), vb)
            new.append((m_new, l_new, acc_new))
        return tuple(new)

    init = tuple((jnp.full((tq, 1), -jnp.inf, F32), jnp.zeros((tq, 1), F32), jnp.zeros((tq, LANES), F32))
                 for _ in range(2))
    n_full = (i * tq) // tk
    carry = lax.fori_loop(0, n_full, lambda j, c: step(pl.multiple_of(j * tk, tk), c, None), init)
    for d in range(tq // tk):
        carry = step(pl.multiple_of(i * tq + d * tk, tk), carry, d * tk)
    o0 = carry[0][2] / carry[0][1]
    o1 = carry[1][2] / carry[1][1]
    o_ref[...] = jnp.where(first, o0, o1).astype(BF16)


def _fox(qkv, c3, cstart):
    s = qkv.shape[0]
    tq = min(FOX_TQ, s)
    nq = s // tq
    kcol = FOX_W // LANES
    return pl.pallas_call(
        _fox_kernel,
        grid=(FOX_PAIRS, nq),
        in_specs=[pl.BlockSpec((tq, LANES), lambda p, i: (i, p)),
                  pl.BlockSpec((s, LANES), lambda p, i: (0, kcol + p)),
                  pl.BlockSpec((s, LANES), lambda p, i: (0, 2 * kcol + p)),
                  pl.BlockSpec((None, 2, s), lambda p, i: (p, 0, 0)),
                  pl.BlockSpec((None, None, 2, LANES), lambda p, i: (p, i, 0, 0))],
        out_specs=pl.BlockSpec((tq, LANES), lambda p, i: (i, p)),
        out_shape=jax.ShapeDtypeStruct((s, FOX_W), BF16),
        compiler_params=pltpu.CompilerParams(dimension_semantics=("arbitrary", "arbitrary"),
                                             vmem_limit_bytes=VMEM_LIMIT),
        name="fox",
    )(qkv, qkv, qkv, c3, cstart)


def _dil_kernel(nb_ref, table_ref, q_ref, kp_ref, kc_ref, vp_ref, vc_ref, bucket_ref, o_ref, bias_ref):
    g = pl.program_id(0)
    n = pl.program_id(1)

    @pl.when(n == 0)
    def _():
        bucket = bucket_ref[...]
        for a in range(2):
            b = jnp.zeros(bucket.shape, F32)
            for t in range(T5_BUCKETS):
                b = jnp.where(bucket == t, table_ref[t * DIL_HEADS + 2 * g + a], b)
            bias_ref[a] = b

    q = q_ref[...]
    kcat = jnp.concatenate([kp_ref[...], kc_ref[...]], axis=0)
    vcat = jnp.concatenate([vp_ref[...], vc_ref[...]], axis=0)
    ii = lax.broadcasted_iota(jnp.int32, (DBLK, 2 * DBLK), 0)
    jj = lax.broadcasted_iota(jnp.int32, (DBLK, 2 * DBLK), 1)
    rel = DBLK + ii - jj
    has_prev = (n % nb_ref[g]) != 0
    mask = (rel >= 0) & (rel <= DBLK) & (has_prev | (jj >= DBLK))
    first = _lane_lt(q.shape, HEAD_DIM)
    outs, lses = [], []
    for a in range(2):
        qa = jnp.where(first if a == 0 else jnp.logical_not(first), q, jnp.zeros_like(q))
        s = _dot_nt(qa, kcat) + bias_ref[a]
        s = jnp.where(mask, s, -jnp.inf)
        m = jnp.max(s, axis=-1, keepdims=True)
        e = jnp.exp(s - m)
        den = jnp.sum(e, axis=-1, keepdims=True)
        outs.append(_dot(e.astype(BF16), vcat) / den)
        lses.append(jnp.broadcast_to(m + jnp.log(den), (DBLK, LANES)))
    o_ref[:, 0:LANES] = jnp.where(first, outs[0], outs[1])
    o_ref[:, LANES:2 * LANES] = jnp.where(first, lses[0], lses[1])


def _dil(nb, table_flat, qd, kd, vd, bucket):
    s = qd.shape[1]
    blk = lambda prev: pl.BlockSpec(
        (None, DBLK, LANES),
        (lambda g, n, *_: (g, jnp.maximum(n - 1, 0), 0)) if prev else (lambda g, n, *_: (g, n, 0)))
    return pl.pallas_call(
        _dil_kernel,
        grid_spec=pltpu.PrefetchScalarGridSpec(
            num_scalar_prefetch=1,
            grid=(DIL_GROUPS, s // DBLK),
            in_specs=[pl.BlockSpec(memory_space=pltpu.SMEM),
                      blk(False), blk(True), blk(False), blk(True), blk(False),
                      pl.BlockSpec((None, DBLK, 2 * DBLK), lambda g, n, *_: (g, 0, 0))],
            out_specs=pl.BlockSpec((None, DBLK, 2 * LANES), lambda g, n, *_: (g, n, 0)),
            scratch_shapes=[pltpu.VMEM((2, DBLK, 2 * DBLK), F32)]),
        out_shape=jax.ShapeDtypeStruct((DIL_GROUPS, s, 2 * LANES), F32),
        compiler_params=pltpu.CompilerParams(dimension_semantics=("arbitrary", "arbitrary"),
                                             vmem_limit_bytes=VMEM_LIMIT),
        name="dil",
    )(nb, table_flat, qd, kd, kd, vd, vd, bucket)


def _t5_bucket(dist):
    is_small = dist < T5_MAX_EXACT
    nf = jnp.maximum(dist, T5_MAX_EXACT).astype(F32)
    large = T5_MAX_EXACT + (jnp.log(nf / T5_MAX_EXACT) / np.log(T5_MAX_DISTANCE / T5_MAX_EXACT)
                            * (T5_BUCKETS - T5_MAX_EXACT)).astype(jnp.int32)
    large = jnp.minimum(large, T5_BUCKETS - 1)
    return jnp.where(is_small, dist, large)


def _merge_kernel(x_ref, ofox_ref, od_ref, omem_ref, wg_ref, wbf_ref, wbd_ref, wbm_ref, wout_ref,
                  g_ref, b_ref, wr_ref, br_ref, h_ref, ri_ref, rg_ref, cnt_ref, carry_ref):
    i = pl.program_id(0)
    tm = x_ref.shape[0]

    @pl.when(i == 0)
    def _():
        carry_ref[...] = jnp.zeros_like(carry_ref)

    x = x_ref[...]
    xb = x.astype(BF16)

    def gate(br):
        return jax.nn.sigmoid(_dot(xb, wg_ref[:, br * D_MODEL:(br + 1) * D_MODEL]))

    merged = gate(0) * _dot(ofox_ref[...], wbf_ref[...])

    lse = [od_ref[g, :, LANES:2 * LANES] for g in range(DIL_GROUPS)]
    mx = jnp.maximum(jnp.maximum(lse[0], lse[1]), lse[2])
    ex = [jnp.exp(l - mx) for l in lse]
    den = ex[0] + ex[1] + ex[2]
    bd = None
    for g in range(DIL_GROUPS):
        og = (od_ref[g, :, 0:LANES] * (ex[g] / den)).astype(BF16)
        t = _dot(og, wbd_ref[g * LANES:(g + 1) * LANES, :])
        bd = t if bd is None else bd + t
    merged = merged + gate(1) * bd
    merged = merged + gate(2) * _dot(omem_ref[...], wbm_ref[...])

    y = _dot(merged.astype(BF16), wout_ref[...])
    h = _layer_norm(DEEPNORM_ALPHA * x + y, g_ref[...], b_ref[...])
    h_ref[...] = h

    lane = lax.broadcasted_iota(jnp.int32, (tm, LANES), 1)
    logits = _dot(h.astype(BF16), wr_ref[...]) + br_ref[...]
    vals = jnp.where(lane < N_EXPERTS, logits, -jnp.inf)
    tops, hots = [], []
    for _ in range(TOP_K):
        mk = jnp.max(vals, axis=-1, keepdims=True)
        ik = jnp.min(jnp.where(vals == mk, lane, LANES), axis=-1, keepdims=True)
        hot = lane == ik
        vals = jnp.where(hot, -jnp.inf, vals)
        tops.append((mk, ik))
        hots.append(hot)
    es = [jnp.exp(mk - tops[0][0]) for mk, _ in tops]
    esum = es[0] + es[1] + es[2] + es[3]

    picked = jnp.where(hots[0] | hots[1] | hots[2] | hots[3], 1.0, 0.0)
    row = lax.broadcasted_iota(jnp.int32, (tm, tm), 0)
    col = lax.broadcasted_iota(jnp.int32, (tm, tm), 1)
    lower = jnp.where(col < row, 1.0, 0.0).astype(BF16)
    before = _dot(lower, picked.astype(BF16)) + carry_ref[0:1, :]
    ri = jnp.zeros((tm, LANES), jnp.int32)
    rg = jnp.zeros((tm, LANES), F32)
    for k in range(TOP_K):
        rank = jnp.sum(jnp.where(hots[k], before, 0.0), axis=-1, keepdims=True)
        ri = jnp.where(lane == k, tops[k][1], ri)
        ri = jnp.where(lane == TOP_K + k, rank.astype(jnp.int32), ri)
        rg = jnp.where(lane == k, es[k] / esum, rg)
    ri_ref[...] = ri
    rg_ref[...] = rg
    total = carry_ref[...] + jnp.sum(picked, axis=0, keepdims=True)
    carry_ref[...] = total
    cnt_ref[...] = total


def _merge(x2, ofox, od, omem, wg, wbf, wbd, wbm, wout, g1, b1, wr, br):
    s = x2.shape[0]
    tm = min(MERGE_TM, s)
    full = lambda a: pl.BlockSpec(a.shape, lambda i: (0,) * a.ndim)
    rows = lambda w: pl.BlockSpec((tm, w), lambda i: (i, 0))
    return pl.pallas_call(
        _merge_kernel,
        grid=(s // tm,),
        in_specs=[rows(D_MODEL), rows(FOX_W),
                  pl.BlockSpec((DIL_GROUPS, tm, 2 * LANES), lambda i: (0, i, 0)),
                  rows(MEM_W), full(wg), full(wbf), full(wbd), full(wbm), full(wout),
                  full(g1), full(b1), full(wr), full(br)],
        out_specs=[rows(D_MODEL), rows(LANES), rows(LANES), pl.BlockSpec((8, LANES), lambda i: (0, 0))],
        out_shape=[jax.ShapeDtypeStruct((s, D_MODEL), F32),
                   jax.ShapeDtypeStruct((s, LANES), jnp.int32),
                   jax.ShapeDtypeStruct((s, LANES), F32),
                   jax.ShapeDtypeStruct((8, LANES), F32)],
        scratch_shapes=[pltpu.VMEM((8, LANES), F32)],
        compiler_params=pltpu.CompilerParams(dimension_semantics=("arbitrary",),
                                             vmem_limit_bytes=VMEM_LIMIT),
        name="merge",
    )(x2, ofox, od, omem, wg, wbf, wbd, wbm, wout, g1, b1, wr, br)


def _row_copy(src_ref, src_row, dst_ref, dst_row, sem):
    return pltpu.make_async_copy(src_ref.at[pl.ds(src_row, 1), :], dst_ref.at[pl.ds(dst_row, 1), :], sem)


def _dispatch_kernel(e_ref, r_ref, ps_ref, h_ref, xb_in_ref, xb_ref, sem):
    del xb_in_ref
    n = e_ref.shape[0]

    def start(a, _):
        dst = ps_ref[e_ref[a]] + r_ref[a]
        _row_copy(h_ref, a // TOP_K, xb_ref, dst, sem).start()
        return 0

    lax.fori_loop(0, n, start, 0)

    def wait(a, _):
        _row_copy(h_ref, 0, xb_ref, 0, sem).wait()
        return 0

    lax.fori_loop(0, n, wait, 0)


def _dispatch(e_flat, r_flat, pad_starts, h, xb0):
    s = h.shape[0]
    tm = min(DISPATCH_TM, s)
    smem1 = lambda n: pl.BlockSpec((n,), lambda i: (i,), memory_space=pltpu.SMEM)
    return pl.pallas_call(
        _dispatch_kernel,
        grid=(s // tm,),
        in_specs=[smem1(tm * TOP_K), smem1(tm * TOP_K),
                  pl.BlockSpec(memory_space=pltpu.SMEM),
                  pl.BlockSpec((tm, D_MODEL), lambda i: (i, 0)),
                  pl.BlockSpec(memory_space=pl.ANY)],
        out_specs=pl.BlockSpec(memory_space=pl.ANY),
        out_shape=jax.ShapeDtypeStruct(xb0.shape, xb0.dtype),
        scratch_shapes=[pltpu.SemaphoreType.DMA(())],
        input_output_aliases={4: 0},
        compiler_params=pltpu.CompilerParams(dimension_semantics=("arbitrary",),
                                             vmem_limit_bytes=VMEM_LIMIT),
        name="dispatch",
    )(e_flat, r_flat, pad_starts, h, xb0)


def _expert_kernel(te_ref, nu_ref, x_ref, w1_ref, b1_ref, w2_ref, b2_ref, y_ref, w1b_ref, w2b_ref):
    b = pl.program_id(0)

    @pl.when(b >= nu_ref[0])
    def _():
        y_ref[...] = jnp.zeros(y_ref.shape, y_ref.dtype)

    @pl.when(b < nu_ref[0])
    def _():
        prev = te_ref[jnp.maximum(b - 1, 0)]

        @pl.when((b == 0) | (te_ref[b] != prev))
        def _():
            chunk = 128
            def cast(r, _):
                rows = pl.ds(pl.multiple_of(r * chunk, chunk), chunk)
                w1b_ref[rows, :] = w1_ref[rows, :].astype(BF16)
                w2b_ref[rows, :] = w2_ref[rows, :].astype(BF16)
                return 0
            lax.fori_loop(0, D_MODEL // chunk, cast, 0)

        xe = x_ref[...].astype(BF16)
        hcat = _dot(xe, w1b_ref[...]) + b1_ref[...]
        h_glu = jnp.minimum(hcat[:, :D_FF], SWIGLU_LIMIT)
        h_lin = jnp.clip(hcat[:, D_FF:], -SWIGLU_LIMIT, SWIGLU_LIMIT)
        act = h_glu * jax.nn.sigmoid(SWIGLU_ALPHA * h_glu) * (h_lin + 1.0)
        y_ref[...] = _dot(act.astype(BF16), w2b_ref[...]) + b2_ref[...]


def _experts(tile_e, n_used, xb, w1, b1, w2, b2):
    p = xb.shape[0]
    tm = EXPERT_TM
    row_map = lambda b, te, nu: (jnp.minimum(b, nu[0] - 1), 0)
    exp_map = lambda b, te, nu: (te[b], 0, 0)
    return pl.pallas_call(
        _expert_kernel,
        grid_spec=pltpu.PrefetchScalarGridSpec(
            num_scalar_prefetch=2,
            grid=(p // tm,),
            in_specs=[pl.BlockSpec((tm, D_MODEL), row_map),
                      pl.BlockSpec((None, D_MODEL, 2 * D_FF), exp_map),
                      pl.BlockSpec((None, 1, 2 * D_FF), exp_map),
                      pl.BlockSpec((None, D_FF, D_MODEL), exp_map),
                      pl.BlockSpec((None, 1, D_MODEL), exp_map)],
            out_specs=pl.BlockSpec((tm, D_MODEL), lambda b, te, nu: (b, 0)),
            scratch_shapes=[pltpu.VMEM((D_MODEL, 2 * D_FF), BF16), pltpu.VMEM((D_FF, D_MODEL), BF16)]),
        out_shape=jax.ShapeDtypeStruct((p, D_MODEL), F32),
        compiler_params=pltpu.CompilerParams(dimension_semantics=("arbitrary",),
                                             vmem_limit_bytes=VMEM_LIMIT),
        name="experts",
    )(tile_e, n_used, xb, w1, b1, w2, b2)


def _combine_kernel(e_ref, r_ref, ps_ref, h_ref, rg_ref, g_ref, b_ref, yb_ref, o_ref, buf_ref, sem):
    tm = h_ref.shape[0]
    n = e_ref.shape[0]

    def start(a, _):
        src = ps_ref[e_ref[a]] + r_ref[a]
        _row_copy(yb_ref, src, buf_ref, (a % TOP_K) * tm + a // TOP_K, sem).start()
        return 0

    lax.fori_loop(0, n, start, 0)

    def wait(a, _):
        _row_copy(yb_ref, 0, buf_ref, 0, sem).wait()
        return 0

    lax.fori_loop(0, n, wait, 0)

    h = h_ref[...]
    rg = rg_ref[...]
    moe = None
    for k in range(TOP_K):
        t = rg[:, k:k + 1] * buf_ref[k * tm:(k + 1) * tm, :]
        moe = t if moe is None else moe + t
    o_ref[...] = _layer_norm(DEEPNORM_ALPHA * h + moe, g_ref[...], b_ref[...])


def _combine(e_flat, r_flat, pad_starts, h, rg, g2, b2, yb):
    s = h.shape[0]
    tm = min(COMBINE_TM, s)
    smem1 = lambda n: pl.BlockSpec((n,), lambda i: (i,), memory_space=pltpu.SMEM)
    full = lambda a: pl.BlockSpec(a.shape, lambda i: (0,) * a.ndim)
    return pl.pallas_call(
        _combine_kernel,
        grid=(s // tm,),
        in_specs=[smem1(tm * TOP_K), smem1(tm * TOP_K),
                  pl.BlockSpec(memory_space=pltpu.SMEM),
                  pl.BlockSpec((tm, D_MODEL), lambda i: (i, 0)),
                  pl.BlockSpec((tm, LANES), lambda i: (i, 0)),
                  full(g2), full(b2),
                  pl.BlockSpec(memory_space=pl.ANY)],
        out_specs=pl.BlockSpec((tm, D_MODEL), lambda i: (i, 0)),
        out_shape=jax.ShapeDtypeStruct((s, D_MODEL), F32),
        scratch_shapes=[pltpu.VMEM((TOP_K * tm, D_MODEL), F32), pltpu.SemaphoreType.DMA(())],
        compiler_params=pltpu.CompilerParams(dimension_semantics=("arbitrary",),
                                             vmem_limit_bytes=VMEM_LIMIT),
        name="combine",
    )(e_flat, r_flat, pad_starts, h, rg, g2, b2, yb)


def _to_subseq(a, dil):
    s, w = a.shape
    return a.reshape(s // dil, dil, w).transpose(1, 0, 2).reshape(s, w)


def _from_subseq(a, dil):
    s, w = a.shape
    return a.reshape(dil, s // dil, w).transpose(1, 0, 2).reshape(s, w)


def kernel(x, mem, w_in, b_fgate, t5_bias, w_mem_kv, w_br_fox, w_br_dil, w_br_mem, w_out, ln1_g, ln1_b,
           w_router, b_router, w_exp_in, b_exp_in, w_exp_out, b_exp_out, ln2_g, ln2_b):
    bsz, s, d = x.shape
    assert bsz == 1 and d == D_MODEL and w_in.shape[0] == 1
    assert s % (DIL_CONFIGS[-1][1] * DBLK) == 0
    x2 = x[0]
    w_in0 = w_in[0]

    w_qkv = jnp.concatenate([w_in0[:, OFF_FOX_QKV:OFF_FOX_F], w_in0[:, OFF_DIL_QKV:OFF_GATES]],
                            axis=1).astype(BF16)
    wft = jnp.zeros((8, D_MODEL), F32).at[:FOX_HEADS].set(w_in0[:, OFF_FOX_F:OFF_DIL_QKV].T).astype(BF16)
    bf = jnp.zeros((8, 1), F32).at[:FOX_HEADS, 0].set(b_fgate[0])
    w_gates = w_in0[:, OFF_GATES:].astype(BF16)

    qkv, c8, o_mem = _proj(x2, w_qkv, wft, bf, mem[0].astype(BF16), w_mem_kv[0].astype(BF16))

    tq = min(FOX_TQ, s)
    c3 = c8[:FOX_HEADS].reshape(FOX_PAIRS, 2, s)
    cstart = jnp.broadcast_to(c3[:, :, ::tq].transpose(0, 2, 1)[..., None], (FOX_PAIRS, s // tq, 2, LANES))
    o_fox = _fox(qkv, c3, cstart)

    qs, ks, vs, bks, nbs = [], [], [], [], []
    ii = np.arange(DBLK, dtype=np.int32)[:, None]
    jj = np.arange(2 * DBLK, dtype=np.int32)[None, :]
    rel = np.clip(DBLK + ii - jj, 0, None)
    for g, (window, dil) in enumerate(DIL_CONFIGS):
        assert window // dil == DBLK
        col = lambda part: qkv[:, QKV_DIL + part * DIL_W + g * LANES:QKV_DIL + part * DIL_W + (g + 1) * LANES]
        qs.append(_to_subseq(col(0), dil))
        ks.append(_to_subseq(col(1), dil))
        vs.append(_to_subseq(col(2), dil))
        bks.append(_t5_bucket(jnp.asarray(rel * dil, dtype=jnp.int32)))
        nbs.append(s // (dil * DBLK))
    od = _dil(jnp.asarray(nbs, jnp.int32), t5_bias.reshape(-1), jnp.stack(qs), jnp.stack(ks), jnp.stack(vs),
              jnp.stack(bks))
    od = jnp.stack([_from_subseq(od[g], dil) for g, (_, dil) in enumerate(DIL_CONFIGS)])

    wr = jnp.zeros((D_MODEL, LANES), F32).at[:, :N_EXPERTS].set(w_router[0]).astype(BF16)
    br = jnp.zeros((1, LANES), F32).at[0, :N_EXPERTS].set(b_router[0])
    h1, ri, rg, cnt = _merge(x2, o_fox, od, o_mem, w_gates, w_br_fox[0].astype(BF16),
                             w_br_dil[0].astype(BF16), w_br_mem[0].astype(BF16), w_out[0].astype(BF16),
                             ln1_g, ln1_b, wr, br)

    tm = EXPERT_TM
    counts = cnt[0, :N_EXPERTS].astype(jnp.int32)
    padded = (counts + tm - 1) // tm * tm
    pad_ends = jnp.cumsum(padded)
    pad_starts = (pad_ends - padded).astype(jnp.int32)
    n_tiles = (s * TOP_K) // tm + N_EXPERTS
    n_used = (pad_ends[-1] // tm).astype(jnp.int32)
    tile_e = jnp.minimum(jnp.searchsorted(pad_ends, jnp.arange(n_tiles, dtype=jnp.int32) * tm, side='right'),
                         N_EXPERTS - 1).astype(jnp.int32)
    tile_e = tile_e[jnp.minimum(jnp.arange(n_tiles), n_used - 1)]
    e_flat = ri[:, 0:TOP_K].reshape(-1)
    r_flat = ri[:, TOP_K:2 * TOP_K].reshape(-1)

    xb = _dispatch(e_flat, r_flat, pad_starts, h1, jnp.zeros((n_tiles * tm, D_MODEL), F32))
    yb = _experts(tile_e, n_used.reshape(1), xb, w_exp_in[0], b_exp_in[0][:, None, :], w_exp_out[0],
                  b_exp_out[0][:, None, :])
    out = _combine(e_flat, r_flat, pad_starts, h1, rg, ln2_g, ln2_b, yb)
    return out[None]
```

```python
import numpy as np
import jax
import jax.numpy as jnp
from jax import lax
from jax.experimental import pallas as pl
from jax.experimental.pallas import tpu as pltpu

D_MODEL = 1024
HEAD_DIM = 64
LANES = 128
SUBLANES = 8
assert D_MODEL == SUBLANES * LANES
FOX_HEADS = 6
FOX_PAIRS = FOX_HEADS // 2
DIL_CONFIGS = ((128, 1), (512, 4), (2048, 16))
DIL_GROUPS = len(DIL_CONFIGS)
DIL_HEADS = 2 * DIL_GROUPS
MEM_HEADS = 4
MEM_PAIRS = MEM_HEADS // 2
MEM_LEN = 256
FOX_W = FOX_HEADS * HEAD_DIM
DIL_W = DIL_HEADS * HEAD_DIM
MEM_W = MEM_HEADS * HEAD_DIM
DBLK = 128
T5_BUCKETS = 32
T5_MAX_EXACT = T5_BUCKETS // 2
T5_MAX_DISTANCE = 2048
N_EXPERTS = 32
TOP_K = 4
D_FF = D_MODEL
SWIGLU_LIMIT = 7.0
SWIGLU_ALPHA = 1.702
LN_EPS = 1e-5
DEEPNORM_ALPHA = 2.0 ** 0.25
QK_SCALE = HEAD_DIM ** -0.5

OFF_FOX_QKV = 0
OFF_FOX_F = OFF_FOX_QKV + 3 * FOX_W
OFF_DIL_QKV = OFF_FOX_F + FOX_HEADS
OFF_MEM_Q = OFF_DIL_QKV + 3 * DIL_W
OFF_GATES = OFF_MEM_Q + MEM_W

W_FOXQ = 0
W_FOXK = FOX_W
W_DIL = 2 * FOX_W
W_MEMQ = W_DIL + 3 * DIL_W
W_FGATE = W_MEMQ + MEM_W
W_ALL = W_FGATE + LANES
FG_PAIR_STRIDE = 8
FG_PIECES = 3
VT_ROWS = LANES + 16

VMEM_LIMIT = 56 * 1024 * 1024

PROJ_TM = 512
FOX_TQ = 1024
FOX_TK = 512
MERGE_TM = 256
DISPATCH_TM = 256
EXPERT_TM = 256
COMBINE_TM = 256
ROW_DMA_UNROLL = 4

BF16 = jnp.bfloat16
F32 = jnp.float32


def _dot(a, b):
    return jnp.dot(a, b, preferred_element_type=F32)


def _dot_nt(a, b):
    return lax.dot_general(a, b, (((1,), (1,)), ((), ())), preferred_element_type=F32)


def _lane_lt(shape, n):
    return lax.broadcasted_iota(jnp.int32, shape, len(shape) - 1) < n


def _split3(v):
    hi = v.astype(BF16)
    r1 = v - hi.astype(F32)
    mid = r1.astype(BF16)
    lo = (r1 - mid.astype(F32)).astype(BF16)
    return hi, mid, lo


def _store_token_tiles(ref, v):
    for c in range(SUBLANES):
        ref[:, c, :] = v[:, c * LANES:(c + 1) * LANES]


def _load_token_tiles(ref, lo, n):
    return jnp.concatenate([ref[lo:lo + n, c, :] for c in range(SUBLANES)], axis=1)


def _layer_norm(r, g, b):
    mu = jnp.mean(r, axis=-1, keepdims=True)
    d = r - mu
    var = jnp.mean(d * d, axis=-1, keepdims=True)
    return d * lax.rsqrt(var + LN_EPS) * g + b


def _proj_kernel(x_ref, w_ref, wvt_ref, bf_ref, mem_ref, wkv_ref, qkv_ref, ext_ref, vt_ref, omem_ref,
                 kvm_ref, carry_ref):
    i = pl.program_id(0)
    tm = x_ref.shape[0]

    @pl.when(i == 0)
    def _():
        kvm_ref[...] = _dot(mem_ref[...], wkv_ref[...]).astype(BF16)
        carry_ref[...] = jnp.zeros_like(carry_ref)

    xb = x_ref[...].astype(BF16)
    proj = _dot(xb, w_ref[...])

    def put(lo, hi, scale):
        v = proj[:, lo:hi]
        if scale is not None:
            v = v * scale
        qkv_ref[:, lo:hi] = v.astype(BF16)

    put(W_FOXQ, W_FOXK, QK_SCALE)
    put(W_FOXK, W_DIL, None)
    put(W_DIL, W_DIL + DIL_W, QK_SCALE)
    put(W_DIL + DIL_W, W_MEMQ, None)

    vt = _dot_nt(wvt_ref[...], xb)
    for p in range(FOX_PAIRS):
        vt_ref[p, 0:LANES, :] = vt[p * LANES:(p + 1) * LANES, :].astype(BF16)
        vt_ref[p, LANES:VT_ROWS, :] = jnp.ones((VT_ROWS - LANES, tm), BF16)

    z = proj[:, W_FGATE:W_ALL] + bf_ref[...]
    logf = jnp.minimum(z, 0.0) - jnp.log1p(jnp.exp(-jnp.abs(z)))
    row = lax.broadcasted_iota(jnp.int32, (tm, tm), 0)
    col = lax.broadcasted_iota(jnp.int32, (tm, tm), 1)
    lower = jnp.where(col <= row, 1.0, 0.0).astype(BF16)
    hi, mid, lo = _split3(logf)
    c = (_dot(lower, lo) + _dot(lower, mid)) + _dot(lower, hi) + carry_ref[0:1, :]
    carry_ref[...] = jnp.broadcast_to(c[tm - 1:tm, :], carry_ref.shape)
    nhi, nmid, nlo = _split3(-c)
    lane = lax.broadcasted_iota(jnp.int32, (tm, LANES), 1)
    slot = lane & (FG_PAIR_STRIDE - 1)
    piece = jnp.where(slot >= FG_PIECES, slot - FG_PIECES, slot)
    used = (slot < 2 * FG_PIECES) & (lane < FOX_PAIRS * FG_PAIR_STRIDE)
    ext = jnp.where(piece == 0, nhi, jnp.where(piece == 1, nmid, nlo))
    ext_ref[...] = jnp.where(used, ext, jnp.zeros_like(ext))

    qm = (proj[:, W_MEMQ:W_FGATE] * QK_SCALE).astype(BF16)
    for p in range(MEM_PAIRS):
        qp = qm[:, p * LANES:(p + 1) * LANES]
        kp = kvm_ref[:, p * LANES:(p + 1) * LANES]
        vp = kvm_ref[:, MEM_W + p * LANES:MEM_W + (p + 1) * LANES]
        first = _lane_lt(qp.shape, HEAD_DIM)
        outs = []
        for a in range(2):
            qa = jnp.where(first if a == 0 else jnp.logical_not(first), qp, jnp.zeros_like(qp))
            s = _dot_nt(qa, kp)
            m = jnp.max(s, axis=-1, keepdims=True)
            e = jnp.exp(s - m)
            den = jnp.sum(e, axis=-1, keepdims=True)
            outs.append(_dot(e.astype(BF16), vp) / den)
        omem_ref[:, p * LANES:(p + 1) * LANES] = jnp.where(first, outs[0], outs[1]).astype(BF16)


def _proj(x2, w_all, wvt, bf, mem_b, wkv_b):
    s = x2.shape[0]
    tm = min(PROJ_TM, s)
    full = lambda shape: pl.BlockSpec(shape, lambda i: (0,) * len(shape))
    return pl.pallas_call(
        _proj_kernel,
        grid=(s // tm,),
        in_specs=[pl.BlockSpec((tm, D_MODEL), lambda i: (i, 0)),
                  full(w_all.shape), full(wvt.shape), full(bf.shape), full(mem_b.shape), full(wkv_b.shape)],
        out_specs=[pl.BlockSpec((tm, W_MEMQ), lambda i: (i, 0)),
                   pl.BlockSpec((tm, LANES), lambda i: (i, 0)),
                   pl.BlockSpec((FOX_PAIRS, VT_ROWS, tm), lambda i: (0, 0, i)),
                   pl.BlockSpec((tm, MEM_W), lambda i: (i, 0))],
        out_shape=[jax.ShapeDtypeStruct((s, W_MEMQ), BF16),
                   jax.ShapeDtypeStruct((s, LANES), BF16),
                   jax.ShapeDtypeStruct((FOX_PAIRS, VT_ROWS, s), BF16),
                   jax.ShapeDtypeStruct((s, MEM_W), BF16)],
        scratch_shapes=[pltpu.VMEM((MEM_LEN, 2 * MEM_W), BF16), pltpu.VMEM((8, LANES), F32)],
        compiler_params=pltpu.CompilerParams(dimension_semantics=("arbitrary",),
                                             vmem_limit_bytes=VMEM_LIMIT),
        name="proj",
    )(x2, w_all, wvt, bf, mem_b, wkv_b)


def _fox_kernel(q_ref, k_ref, e_ref, vt_ref, o_ref, m_ref, acc_ref):
    p = pl.program_id(0)
    i = pl.program_id(1)
    t = q_ref.shape[0]
    tk = min(FOX_TK, t)
    q = q_ref[...]
    lane = lax.broadcasted_iota(jnp.int32, (t, LANES), 1)
    qx = []
    for a in range(2):
        head = (lane < HEAD_DIM) if a == 0 else (lane >= HEAD_DIM)
        lo = FG_PAIR_STRIDE * p + FG_PIECES * a
        ones = jnp.where((lane >= lo) & (lane < lo + FG_PIECES), 1.0, 0.0).astype(BF16)
        qx.append(jnp.concatenate([jnp.where(head, q, jnp.zeros_like(q)), ones], axis=1))

    m_ref[...] = jnp.full(m_ref.shape, -jnp.inf, F32)
    acc_ref[...] = jnp.zeros(acc_ref.shape, F32)

    def step(start, q0):
        kx = jnp.concatenate([k_ref[pl.ds(start, tk), :], e_ref[pl.ds(start, tk), :]], axis=1)
        vtb = vt_ref[:, pl.ds(start, tk)]
        scores = [_dot_nt(kx, qx[a][q0:, :]) for a in range(2)]
        for a in range(2):
            s = scores[a]
            if q0 is not None:
                kpos = lax.broadcasted_iota(jnp.int32, s.shape, 0)
                qpos = lax.broadcasted_iota(jnp.int32, s.shape, 1)
                s = jnp.where(kpos <= qpos, s, -jnp.inf)
            qs = slice(q0, t)
            m_prev = m_ref[a, :, qs]
            m_new = jnp.maximum(m_prev, jnp.max(s, axis=0, keepdims=True))
            alpha = jnp.exp(m_prev - m_new)
            pt = jnp.exp(s - m_new).astype(BF16)
            m_ref[a, :, qs] = m_new
            acc_ref[a, :, qs] = alpha * acc_ref[a, :, qs] + _dot(vtb, pt)

    def full_step(j, _):
        step(pl.multiple_of(j * tk, tk), None)
        return 0

    lax.fori_loop(0, i * (t // tk), full_step, 0)
    for d in range(t // tk):
        step(pl.multiple_of(i * t + d * tk, tk), d * tk)
    o0 = acc_ref[0, 0:HEAD_DIM, :] / acc_ref[0, LANES:LANES + 1, :]
    o1 = acc_ref[1, HEAD_DIM:LANES, :] / acc_ref[1, LANES:LANES + 1, :]
    o_ref[...] = jnp.concatenate([o0, o1], axis=0).T.astype(BF16)


def _fox(qkv, ext, vt):
    s = qkv.shape[0]
    t = min(FOX_TQ, s)
    assert s % t == 0 and t % min(FOX_TK, t) == 0
    kcol = W_FOXK // LANES
    return pl.pallas_call(
        _fox_kernel,
        grid=(FOX_PAIRS, s // t),
        in_specs=[pl.BlockSpec((t, LANES), lambda p, i: (i, p)),
                  pl.BlockSpec((s, LANES), lambda p, i: (0, kcol + p)),
                  pl.BlockSpec((s, LANES), lambda p, i: (0, 0)),
                  pl.BlockSpec((None, VT_ROWS, s), lambda p, i: (p, 0, 0))],
        out_specs=pl.BlockSpec((t, LANES), lambda p, i: (i, p)),
        out_shape=jax.ShapeDtypeStruct((s, FOX_W), BF16),
        scratch_shapes=[pltpu.VMEM((2, 1, t), F32), pltpu.VMEM((2, VT_ROWS, t), F32)],
        compiler_params=pltpu.CompilerParams(dimension_semantics=("arbitrary", "arbitrary"),
                                             vmem_limit_bytes=VMEM_LIMIT),
        name="fox",
    )(qkv, qkv, ext, vt)


def _dil_kernel(nb_ref, table_ref, q_ref, kp_ref, kc_ref, vp_ref, vc_ref, bucket_ref, o_ref, bias_ref):
    g = pl.program_id(0)
    n = pl.program_id(1)

    @pl.when(n == 0)
    def _():
        bucket = bucket_ref[...]
        for a in range(2):
            b = jnp.zeros(bucket.shape, F32)
            for t in range(T5_BUCKETS):
                b = jnp.where(bucket == t, table_ref[t * DIL_HEADS + 2 * g + a], b)
            bias_ref[a] = b

    q = q_ref[...]
    kcat = jnp.concatenate([kp_ref[...], kc_ref[...]], axis=0)
    vcat = jnp.concatenate([vp_ref[...], vc_ref[...]], axis=0)
    ii = lax.broadcasted_iota(jnp.int32, (DBLK, 2 * DBLK), 0)
    jj = lax.broadcasted_iota(jnp.int32, (DBLK, 2 * DBLK), 1)
    rel = DBLK + ii - jj
    has_prev = (n % nb_ref[g]) != 0
    mask = (rel >= 0) & (rel <= DBLK) & (has_prev | (jj >= DBLK))
    first = _lane_lt(q.shape, HEAD_DIM)
    outs, lses = [], []
    for a in range(2):
        qa = jnp.where(first if a == 0 else jnp.logical_not(first), q, jnp.zeros_like(q))
        s = _dot_nt(qa, kcat) + bias_ref[a]
        s = jnp.where(mask, s, -jnp.inf)
        m = jnp.max(s, axis=-1, keepdims=True)
        e = jnp.exp(s - m)
        den = jnp.sum(e, axis=-1, keepdims=True)
        outs.append(_dot(e.astype(BF16), vcat) / den)
        lses.append(jnp.broadcast_to(m + jnp.log(den), (DBLK, LANES)))
    o_ref[:, 0:LANES] = jnp.where(first, outs[0], outs[1])
    o_ref[:, LANES:2 * LANES] = jnp.where(first, lses[0], lses[1])


def _dil(nb, table_flat, qd, kd, vd, bucket):
    s = qd.shape[1]
    blk = lambda prev: pl.BlockSpec(
        (None, DBLK, LANES),
        (lambda g, n, *_: (g, jnp.maximum(n - 1, 0), 0)) if prev else (lambda g, n, *_: (g, n, 0)))
    return pl.pallas_call(
        _dil_kernel,
        grid_spec=pltpu.PrefetchScalarGridSpec(
            num_scalar_prefetch=1,
            grid=(DIL_GROUPS, s // DBLK),
            in_specs=[pl.BlockSpec(memory_space=pltpu.SMEM),
                      blk(False), blk(True), blk(False), blk(True), blk(False),
                      pl.BlockSpec((None, DBLK, 2 * DBLK), lambda g, n, *_: (g, 0, 0))],
            out_specs=pl.BlockSpec((None, DBLK, 2 * LANES), lambda g, n, *_: (g, n, 0)),
            scratch_shapes=[pltpu.VMEM((2, DBLK, 2 * DBLK), F32)]),
        out_shape=jax.ShapeDtypeStruct((DIL_GROUPS, s, 2 * LANES), F32),
        compiler_params=pltpu.CompilerParams(dimension_semantics=("arbitrary", "arbitrary"),
                                             vmem_limit_bytes=VMEM_LIMIT),
        name="dil",
    )(nb, table_flat, qd, kd, kd, vd, vd, bucket)


def _t5_bucket(dist):
    is_small = dist < T5_MAX_EXACT
    nf = jnp.maximum(dist, T5_MAX_EXACT).astype(F32)
    large = T5_MAX_EXACT + (jnp.log(nf / T5_MAX_EXACT) / np.log(T5_MAX_DISTANCE / T5_MAX_EXACT)
                            * (T5_BUCKETS - T5_MAX_EXACT)).astype(jnp.int32)
    large = jnp.minimum(large, T5_BUCKETS - 1)
    return jnp.where(is_small, dist, large)


def _merge_kernel(x_ref, ofox_ref, od_ref, omem_ref, wg_ref, wbf_ref, wbd_ref, wbm_ref, wout_ref,
                  g_ref, b_ref, wr_ref, br_ref, h_ref, h3_ref, ri_ref, rg_ref, cnt_ref, carry_ref):
    i = pl.program_id(0)
    tm = x_ref.shape[0]

    @pl.when(i == 0)
    def _():
        carry_ref[...] = jnp.zeros_like(carry_ref)

    x = x_ref[...]
    xb = x.astype(BF16)

    def gate(br):
        return jax.nn.sigmoid(_dot(xb, wg_ref[:, br * D_MODEL:(br + 1) * D_MODEL]))

    merged = gate(0) * _dot(ofox_ref[...], wbf_ref[...])

    lse = [od_ref[g, :, LANES:2 * LANES] for g in range(DIL_GROUPS)]
    mx = jnp.maximum(jnp.maximum(lse[0], lse[1]), lse[2])
    ex = [jnp.exp(l - mx) for l in lse]
    den = ex[0] + ex[1] + ex[2]
    bd = None
    for g in range(DIL_GROUPS):
        og = (od_ref[g, :, 0:LANES] * (ex[g] / den)).astype(BF16)
        t = _dot(og, wbd_ref[g * LANES:(g + 1) * LANES, :])
        bd = t if bd is None else bd + t
    merged = merged + gate(1) * bd
    merged = merged + gate(2) * _dot(omem_ref[...], wbm_ref[...])

    y = _dot(merged.astype(BF16), wout_ref[...])
    h = _layer_norm(DEEPNORM_ALPHA * x + y, g_ref[...], b_ref[...])
    h_ref[...] = h
    _store_token_tiles(h3_ref, h)

    lane = lax.broadcasted_iota(jnp.int32, (tm, LANES), 1)
    logits = _dot(h.astype(BF16), wr_ref[...]) + br_ref[...]
    vals = jnp.where(lane < N_EXPERTS, logits, -jnp.inf)
    tops, hots = [], []
    for _ in range(TOP_K):
        mk = jnp.max(vals, axis=-1, keepdims=True)
        ik = jnp.min(jnp.where(vals == mk, lane, LANES), axis=-1, keepdims=True)
        hot = lane == ik
        vals = jnp.where(hot, -jnp.inf, vals)
        tops.append((mk, ik))
        hots.append(hot)
    es = [jnp.exp(mk - tops[0][0]) for mk, _ in tops]
    esum = es[0] + es[1] + es[2] + es[3]

    picked = jnp.where(hots[0] | hots[1] | hots[2] | hots[3], 1.0, 0.0)
    row = lax.broadcasted_iota(jnp.int32, (tm, tm), 0)
    col = lax.broadcasted_iota(jnp.int32, (tm, tm), 1)
    lower = jnp.where(col < row, 1.0, 0.0).astype(BF16)
    before = _dot(lower, picked.astype(BF16)) + carry_ref[0:1, :]
    ri = jnp.zeros((tm, LANES), jnp.int32)
    rg = jnp.zeros((tm, LANES), F32)
    for k in range(TOP_K):
        rank = jnp.sum(jnp.where(hots[k], before, 0.0), axis=-1, keepdims=True)
        ri = jnp.where(lane == k, tops[k][1], ri)
        ri = jnp.where(lane == TOP_K + k, rank.astype(jnp.int32), ri)
        rg = jnp.where(lane == k, es[k] / esum, rg)
    ri_ref[...] = ri
    rg_ref[...] = rg
    total = carry_ref[...] + jnp.sum(picked, axis=0, keepdims=True)
    carry_ref[...] = total
    cnt_ref[...] = total


def _merge(x2, ofox, od, omem, wg, wbf, wbd, wbm, wout, g1, b1, wr, br):
    s = x2.shape[0]
    tm = min(MERGE_TM, s)
    full = lambda a: pl.BlockSpec(a.shape, lambda i: (0,) * a.ndim)
    rows = lambda w: pl.BlockSpec((tm, w), lambda i: (i, 0))
    return pl.pallas_call(
        _merge_kernel,
        grid=(s // tm,),
        in_specs=[rows(D_MODEL), rows(FOX_W),
                  pl.BlockSpec((DIL_GROUPS, tm, 2 * LANES), lambda i: (0, i, 0)),
                  rows(MEM_W), full(wg), full(wbf), full(wbd), full(wbm), full(wout),
                  full(g1), full(b1), full(wr), full(br)],
        out_specs=[rows(D_MODEL), pl.BlockSpec((tm, SUBLANES, LANES), lambda i: (i, 0, 0)),
                   rows(LANES), rows(LANES), pl.BlockSpec((8, LANES), lambda i: (0, 0))],
        out_shape=[jax.ShapeDtypeStruct((s, D_MODEL), F32),
                   jax.ShapeDtypeStruct((s, SUBLANES, LANES), F32),
                   jax.ShapeDtypeStruct((s, LANES), jnp.int32),
                   jax.ShapeDtypeStruct((s, LANES), F32),
                   jax.ShapeDtypeStruct((8, LANES), F32)],
        scratch_shapes=[pltpu.VMEM((8, LANES), F32)],
        compiler_params=pltpu.CompilerParams(dimension_semantics=("arbitrary",),
                                             vmem_limit_bytes=VMEM_LIMIT),
        name="merge",
    )(x2, ofox, od, omem, wg, wbf, wbd, wbm, wout, g1, b1, wr, br)


def _row_copy(src_ref, src_row, dst_ref, dst_row, sem):
    return pltpu.make_async_copy(src_ref.at[pl.ds(src_row, 1)], dst_ref.at[pl.ds(dst_row, 1)], sem)


def _dispatch_kernel(e_ref, r_ref, ps_ref, h_ref, xb_in_ref, xb_ref, sem):
    del xb_in_ref
    tm = h_ref.shape[0]

    def start(t, _):
        for k in range(TOP_K):
            a = t * TOP_K + k
            _row_copy(h_ref, t, xb_ref, ps_ref[e_ref[a]] + r_ref[a], sem).start()
        return 0

    lax.fori_loop(0, tm, start, 0, unroll=ROW_DMA_UNROLL)

    def wait(t, _):
        for k in range(TOP_K):
            _row_copy(h_ref, 0, xb_ref, 0, sem).wait()
        return 0

    lax.fori_loop(0, tm, wait, 0, unroll=ROW_DMA_UNROLL)


def _dispatch(e_flat, r_flat, pad_starts, h, xb0):
    s = h.shape[0]
    tm = min(DISPATCH_TM, s)
    smem1 = lambda n: pl.BlockSpec((n,), lambda i: (i,), memory_space=pltpu.SMEM)
    return pl.pallas_call(
        _dispatch_kernel,
        grid=(s // tm,),
        in_specs=[smem1(tm * TOP_K), smem1(tm * TOP_K),
                  pl.BlockSpec(memory_space=pltpu.SMEM),
                  pl.BlockSpec((tm, SUBLANES, LANES), lambda i: (i, 0, 0)),
                  pl.BlockSpec(memory_space=pl.ANY)],
        out_specs=pl.BlockSpec(memory_space=pl.ANY),
        out_shape=jax.ShapeDtypeStruct(xb0.shape, xb0.dtype),
        scratch_shapes=[pltpu.SemaphoreType.DMA(())],
        input_output_aliases={4: 0},
        compiler_params=pltpu.CompilerParams(dimension_semantics=("arbitrary",),
                                             vmem_limit_bytes=VMEM_LIMIT),
        name="dispatch",
    )(e_flat, r_flat, pad_starts, h, xb0)


def _expert_kernel(te_ref, nu_ref, x_ref, w1_ref, b1_ref, w2_ref, b2_ref, y_ref, w1b_ref, w2b_ref):
    b = pl.program_id(0)

    @pl.when(b >= nu_ref[0])
    def _():
        y_ref[...] = jnp.zeros(y_ref.shape, y_ref.dtype)

    @pl.when(b < nu_ref[0])
    def _():
        prev = te_ref[jnp.maximum(b - 1, 0)]

        @pl.when((b == 0) | (te_ref[b] != prev))
        def _():
            chunk = 128
            def cast(r, _):
                rows = pl.ds(pl.multiple_of(r * chunk, chunk), chunk)
                w1b_ref[rows, :] = w1_ref[rows, :].astype(BF16)
                w2b_ref[rows, :] = w2_ref[rows, :].astype(BF16)
                return 0
            lax.fori_loop(0, D_MODEL // chunk, cast, 0)

        xe = _load_token_tiles(x_ref, 0, x_ref.shape[0]).astype(BF16)
        hcat = _dot(xe, w1b_ref[...]) + b1_ref[...]
        h_glu = jnp.minimum(hcat[:, :D_FF], SWIGLU_LIMIT)
        h_lin = jnp.clip(hcat[:, D_FF:], -SWIGLU_LIMIT, SWIGLU_LIMIT)
        act = h_glu * jax.nn.sigmoid(SWIGLU_ALPHA * h_glu) * (h_lin + 1.0)
        _store_token_tiles(y_ref, _dot(act.astype(BF16), w2b_ref[...]) + b2_ref[...])


def _experts(tile_e, n_used, xb, w1, b1, w2, b2):
    p = xb.shape[0]
    tm = EXPERT_TM
    row_map = lambda b, te, nu: (jnp.minimum(b, nu[0] - 1), 0, 0)
    exp_map = lambda b, te, nu: (te[b], 0, 0)
    return pl.pallas_call(
        _expert_kernel,
        grid_spec=pltpu.PrefetchScalarGridSpec(
            num_scalar_prefetch=2,
            grid=(p // tm,),
            in_specs=[pl.BlockSpec((tm, SUBLANES, LANES), row_map),
                      pl.BlockSpec((None, D_MODEL, 2 * D_FF), exp_map),
                      pl.BlockSpec((None, 1, 2 * D_FF), exp_map),
                      pl.BlockSpec((None, D_FF, D_MODEL), exp_map),
                      pl.BlockSpec((None, 1, D_MODEL), exp_map)],
            out_specs=pl.BlockSpec((tm, SUBLANES, LANES), lambda b, te, nu: (b, 0, 0)),
            scratch_shapes=[pltpu.VMEM((D_MODEL, 2 * D_FF), BF16), pltpu.VMEM((D_FF, D_MODEL), BF16)]),
        out_shape=jax.ShapeDtypeStruct((p, SUBLANES, LANES), F32),
        compiler_params=pltpu.CompilerParams(dimension_semantics=("arbitrary",),
                                             vmem_limit_bytes=VMEM_LIMIT),
        name="experts",
    )(tile_e, n_used, xb, w1, b1, w2, b2)


def _combine_kernel(e_ref, r_ref, ps_ref, h_ref, rg_ref, g_ref, b_ref, yb_ref, o_ref, buf_ref, sem):
    tm = h_ref.shape[0]

    def start(t, _):
        for k in range(TOP_K):
            a = t * TOP_K + k
            _row_copy(yb_ref, ps_ref[e_ref[a]] + r_ref[a], buf_ref, k * tm + t, sem).start()
        return 0

    lax.fori_loop(0, tm, start, 0, unroll=ROW_DMA_UNROLL)

    def wait(t, _):
        for k in range(TOP_K):
            _row_copy(yb_ref, 0, buf_ref, 0, sem).wait()
        return 0

    lax.fori_loop(0, tm, wait, 0, unroll=ROW_DMA_UNROLL)

    h = h_ref[...]
    rg = rg_ref[...]
    moe = None
    for k in range(TOP_K):
        t = rg[:, k:k + 1] * _load_token_tiles(buf_ref, k * tm, tm)
        moe = t if moe is None else moe + t
    o_ref[...] = _layer_norm(DEEPNORM_ALPHA * h + moe, g_ref[...], b_ref[...])


def _combine(e_flat, r_flat, pad_starts, h, rg, g2, b2, yb):
    s = h.shape[0]
    tm = min(COMBINE_TM, s)
    smem1 = lambda n: pl.BlockSpec((n,), lambda i: (i,), memory_space=pltpu.SMEM)
    full = lambda a: pl.BlockSpec(a.shape, lambda i: (0,) * a.ndim)
    return pl.pallas_call(
        _combine_kernel,
        grid=(s // tm,),
        in_specs=[smem1(tm * TOP_K), smem1(tm * TOP_K),
                  pl.BlockSpec(memory_space=pltpu.SMEM),
                  pl.BlockSpec((tm, D_MODEL), lambda i: (i, 0)),
                  pl.BlockSpec((tm, LANES), lambda i: (i, 0)),
                  full(g2), full(b2),
                  pl.BlockSpec(memory_space=pl.ANY)],
        out_specs=pl.BlockSpec((tm, D_MODEL), lambda i: (i, 0)),
        out_shape=jax.ShapeDtypeStruct((s, D_MODEL), F32),
        scratch_shapes=[pltpu.VMEM((TOP_K * tm, SUBLANES, LANES), F32), pltpu.SemaphoreType.DMA(())],
        compiler_params=pltpu.CompilerParams(dimension_semantics=("arbitrary",),
                                             vmem_limit_bytes=VMEM_LIMIT),
        name="combine",
    )(e_flat, r_flat, pad_starts, h, rg, g2, b2, yb)


def _to_subseq(a, dil):
    s, w = a.shape
    return a.reshape(s // dil, dil, w).transpose(1, 0, 2).reshape(s, w)


def _from_subseq(a, dil):
    s, w = a.shape
    return a.reshape(dil, s // dil, w).transpose(1, 0, 2).reshape(s, w)


def kernel(x, mem, w_in, b_fgate, t5_bias, w_mem_kv, w_br_fox, w_br_dil, w_br_mem, w_out, ln1_g, ln1_b,
           w_router, b_router, w_exp_in, b_exp_in, w_exp_out, b_exp_out, ln2_g, ln2_b):
    bsz, s, d = x.shape
    assert bsz == 1 and d == D_MODEL and w_in.shape[0] == 1
    assert s % (DIL_CONFIGS[-1][1] * DBLK) == 0
    x2 = x[0]
    w_in0 = w_in[0]

    fg_lanes = np.array([FG_PAIR_STRIDE * (h // 2) + FG_PIECES * (h % 2) + j
                         for h in range(FOX_HEADS) for j in range(FG_PIECES)])
    fg_heads = np.repeat(np.arange(FOX_HEADS), FG_PIECES)
    w_fg = jnp.zeros((D_MODEL, LANES), F32).at[:, fg_lanes].set(w_in0[:, OFF_FOX_F + fg_heads])
    bf = jnp.zeros((1, LANES), F32).at[0, fg_lanes].set(b_fgate[0][fg_heads])
    w_all = jnp.concatenate([w_in0[:, OFF_FOX_QKV:OFF_FOX_QKV + 2 * FOX_W], w_in0[:, OFF_DIL_QKV:OFF_GATES],
                             w_fg], axis=1).astype(BF16)
    wvt = w_in0[:, OFF_FOX_QKV + 2 * FOX_W:OFF_FOX_F].T.astype(BF16)
    w_gates = w_in0[:, OFF_GATES:].astype(BF16)

    qkv, ext, vt, o_mem = _proj(x2, w_all, wvt, bf, mem[0].astype(BF16), w_mem_kv[0].astype(BF16))

    o_fox = _fox(qkv, ext, vt)

    qs, ks, vs, bks, nbs = [], [], [], [], []
    ii = np.arange(DBLK, dtype=np.int32)[:, None]
    jj = np.arange(2 * DBLK, dtype=np.int32)[None, :]
    rel = np.clip(DBLK + ii - jj, 0, None)
    for g, (window, dil) in enumerate(DIL_CONFIGS):
        assert window // dil == DBLK
        col = lambda part: qkv[:, W_DIL + part * DIL_W + g * LANES:W_DIL + part * DIL_W + (g + 1) * LANES]
        qs.append(_to_subseq(col(0), dil))
        ks.append(_to_subseq(col(1), dil))
        vs.append(_to_subseq(col(2), dil))
        bks.append(_t5_bucket(jnp.asarray(rel * dil, dtype=jnp.int32)))
        nbs.append(s // (dil * DBLK))
    od = _dil(jnp.asarray(nbs, jnp.int32), t5_bias.reshape(-1), jnp.stack(qs), jnp.stack(ks), jnp.stack(vs),
              jnp.stack(bks))
    od = jnp.stack([_from_subseq(od[g], dil) for g, (_, dil) in enumerate(DIL_CONFIGS)])

    wr = jnp.zeros((D_MODEL, LANES), F32).at[:, :N_EXPERTS].set(w_router[0]).astype(BF16)
    br = jnp.zeros((1, LANES), F32).at[0, :N_EXPERTS].set(b_router[0])
    h1, h3, ri, rg, cnt = _merge(x2, o_fox, od, o_mem, w_gates, w_br_fox[0].astype(BF16),
                             w_br_dil[0].astype(BF16), w_br_mem[0].astype(BF16), w_out[0].astype(BF16),
                             ln1_g, ln1_b, wr, br)

    tm = EXPERT_TM
    counts = cnt[0, :N_EXPERTS].astype(jnp.int32)
    padded = (counts + tm - 1) // tm * tm
    pad_ends = jnp.cumsum(padded)
    pad_starts = (pad_ends - padded).astype(jnp.int32)
    n_tiles = (s * TOP_K) // tm + N_EXPERTS
    n_used = (pad_ends[-1] // tm).astype(jnp.int32)
    tile_e = jnp.minimum(jnp.searchsorted(pad_ends, jnp.arange(n_tiles, dtype=jnp.int32) * tm, side='right'),
                         N_EXPERTS - 1).astype(jnp.int32)
    tile_e = tile_e[jnp.minimum(jnp.arange(n_tiles), n_used - 1)]
    e_flat = ri[:, 0:TOP_K].reshape(-1)
    r_flat = ri[:, TOP_K:2 * TOP_K].reshape(-1)

    xb = _dispatch(e_flat, r_flat, pad_starts, h3, jnp.zeros((n_tiles * tm, SUBLANES, LANES), F32))
    yb = _experts(tile_e, n_used.reshape(1), xb, w_exp_in[0], b_exp_in[0][:, None, :], w_exp_out[0],
                  b_exp_out[0][:, None, :])
    out = _combine(e_flat, r_flat, pad_starts, h1, rg, ln2_g, ln2_b, yb)
    return out[None]
```

```python
import functools

import numpy as np
import jax
import jax.numpy as jnp
from jax import lax
from jax.experimental import pallas as pl
from jax.experimental.pallas import tpu as pltpu

D_MODEL = 1024
HEAD_DIM = 64
LANES = 128
SUBLANES = 8
assert D_MODEL == SUBLANES * LANES
FOX_HEADS = 6
FOX_PAIRS = FOX_HEADS // 2
DIL_CONFIGS = ((128, 1), (512, 4), (2048, 16))
DIL_GROUPS = len(DIL_CONFIGS)
DIL_HEADS = 2 * DIL_GROUPS
MEM_HEADS = 4
MEM_PAIRS = MEM_HEADS // 2
MEM_LEN = 256
FOX_W = FOX_HEADS * HEAD_DIM
DIL_W = DIL_HEADS * HEAD_DIM
MEM_W = MEM_HEADS * HEAD_DIM
DBLK = 128
DIL_QKV_W = 3 * LANES
T5_BUCKETS = 32
T5_MAX_EXACT = T5_BUCKETS // 2
T5_MAX_DISTANCE = 2048
N_EXPERTS = 32
TOP_K = 4
D_FF = D_MODEL
SWIGLU_LIMIT = 7.0
SWIGLU_ALPHA = 1.702
LN_EPS = 1e-5
DEEPNORM_ALPHA = 2.0 ** 0.25
QK_SCALE = HEAD_DIM ** -0.5

OFF_FOX_QKV = 0
OFF_FOX_F = OFF_FOX_QKV + 3 * FOX_W
OFF_DIL_QKV = OFF_FOX_F + FOX_HEADS
OFF_MEM_Q = OFF_DIL_QKV + 3 * DIL_W
OFF_GATES = OFF_MEM_Q + MEM_W

W_FOXQ = 0
W_FOXK = FOX_W
W_DIL = 2 * FOX_W
W_MEMQ = W_DIL + 3 * DIL_W
W_FGATE = W_MEMQ + MEM_W
W_ALL = W_FGATE + LANES
FG_PAIR_STRIDE = 8
FG_PIECES = 3
VT_ROWS = LANES + 16

VMEM_LIMIT = 56 * 1024 * 1024

PROJ_TM = 512
FOX_TQ = 1024
FOX_TK = 512
DIL_CHUNK = 512
MERGE_TM = 512
DISPATCH_TM = 256
EXPERT_TM = 256
COMBINE_TM = 256
ROW_DMA_UNROLL = 4

BF16 = jnp.bfloat16
F32 = jnp.float32


def _dot(a, b):
    return jnp.dot(a, b, preferred_element_type=F32)


def _dot_nt(a, b):
    return lax.dot_general(a, b, (((1,), (1,)), ((), ())), preferred_element_type=F32)


def _lane_lt(shape, n):
    return lax.broadcasted_iota(jnp.int32, shape, len(shape) - 1) < n


def _split3(v):
    hi = v.astype(BF16)
    r1 = v - hi.astype(F32)
    mid = r1.astype(BF16)
    lo = (r1 - mid.astype(F32)).astype(BF16)
    return hi, mid, lo


def _store_token_tiles(ref, v):
    n = v.shape[0]
    for c in range(SUBLANES):
        ref[pl.ds(c, n, stride=SUBLANES), :] = v[:, c * LANES:(c + 1) * LANES]


def _load_token_tiles(ref, lo, n):
    return jnp.concatenate([ref[pl.ds(lo * SUBLANES + c, n, stride=SUBLANES), :] for c in range(SUBLANES)],
                           axis=1)


def _layer_norm(r, g, b):
    mu = jnp.mean(r, axis=-1, keepdims=True)
    d = r - mu
    var = jnp.mean(d * d, axis=-1, keepdims=True)
    return d * lax.rsqrt(var + LN_EPS) * g + b


def _proj_kernel(x_ref, w_ref, wvt_ref, bf_ref, mem_ref, wkv_ref, qk_ref, ext_ref, vt_ref, omem_ref,
                 d0_ref, d1_ref, d2_ref, kvm_ref, carry_ref, stage_ref):
    i = pl.program_id(0)
    tm = x_ref.shape[0]

    @pl.when(i == 0)
    def _():
        kvm_ref[...] = _dot(mem_ref[...], wkv_ref[...]).astype(BF16)
        carry_ref[...] = jnp.zeros_like(carry_ref)

    xb = x_ref[...].astype(BF16)
    proj = _dot(xb, w_ref[...])

    qk_ref[:, W_FOXQ:W_FOXK] = (proj[:, W_FOXQ:W_FOXK] * QK_SCALE).astype(BF16)
    qk_ref[:, W_FOXK:W_DIL] = proj[:, W_FOXK:W_DIL].astype(BF16)

    for g, (d_ref, (_, dil)) in enumerate(zip((d0_ref, d1_ref, d2_ref), DIL_CONFIGS)):
        cols = [proj[:, W_DIL + part * DIL_W + g * LANES:W_DIL + part * DIL_W + (g + 1) * LANES]
                for part in range(3)]
        cols[0] = cols[0] * QK_SCALE
        if dil == 1:
            for part in range(3):
                d_ref[0, :, part * LANES:(part + 1) * LANES] = cols[part].astype(BF16)
        else:
            for part in range(3):
                stage_ref[part] = cols[part]
            for r in range(dil):
                for part in range(3):
                    d_ref[r, :, part * LANES:(part + 1) * LANES] = (
                        stage_ref[part, pl.ds(r, tm // dil, stride=dil), :].astype(BF16))

    vt = _dot_nt(wvt_ref[...], xb)
    for p in range(FOX_PAIRS):
        vt_ref[p, 0:LANES, :] = vt[p * LANES:(p + 1) * LANES, :].astype(BF16)
        vt_ref[p, LANES:VT_ROWS, :] = jnp.ones((VT_ROWS - LANES, tm), BF16)

    z = proj[:, W_FGATE:W_ALL] + bf_ref[...]
    logf = jnp.minimum(z, 0.0) - jnp.log1p(jnp.exp(-jnp.abs(z)))
    row = lax.broadcasted_iota(jnp.int32, (tm, tm), 0)
    col = lax.broadcasted_iota(jnp.int32, (tm, tm), 1)
    lower = jnp.where(col <= row, 1.0, 0.0).astype(BF16)
    hi, mid, lo = _split3(logf)
    c = (_dot(lower, lo) + _dot(lower, mid)) + _dot(lower, hi) + carry_ref[0:1, :]
    carry_ref[...] = jnp.broadcast_to(c[tm - 1:tm, :], carry_ref.shape)
    nhi, nmid, nlo = _split3(-c)
    lane = lax.broadcasted_iota(jnp.int32, (tm, LANES), 1)
    slot = lane & (FG_PAIR_STRIDE - 1)
    piece = jnp.where(slot >= FG_PIECES, slot - FG_PIECES, slot)
    used = (slot < 2 * FG_PIECES) & (lane < FOX_PAIRS * FG_PAIR_STRIDE)
    ext = jnp.where(piece == 0, nhi, jnp.where(piece == 1, nmid, nlo))
    ext_ref[...] = jnp.where(used, ext, jnp.zeros_like(ext))

    qm = (proj[:, W_MEMQ:W_FGATE] * QK_SCALE).astype(BF16)
    for p in range(MEM_PAIRS):
        qp = qm[:, p * LANES:(p + 1) * LANES]
        kp = kvm_ref[:, p * LANES:(p + 1) * LANES]
        vp = kvm_ref[:, MEM_W + p * LANES:MEM_W + (p + 1) * LANES]
        first = _lane_lt(qp.shape, HEAD_DIM)
        outs = []
        for a in range(2):
            qa = jnp.where(first if a == 0 else jnp.logical_not(first), qp, jnp.zeros_like(qp))
            s = _dot_nt(qa, kp)
            m = jnp.max(s, axis=-1, keepdims=True)
            e = jnp.exp(s - m)
            den = jnp.sum(e, axis=-1, keepdims=True)
            outs.append(_dot(e.astype(BF16), vp) / den)
        omem_ref[:, p * LANES:(p + 1) * LANES] = jnp.where(first, outs[0], outs[1]).astype(BF16)


def _proj(x2, w_all, wvt, bf, mem_b, wkv_b):
    s = x2.shape[0]
    tm = min(PROJ_TM, s)
    full = lambda shape: pl.BlockSpec(shape, lambda i: (0,) * len(shape))
    rows = lambda w: pl.BlockSpec((tm, w), lambda i: (i, 0))
    dil_spec = lambda dil: pl.BlockSpec((dil, tm // dil, DIL_QKV_W), lambda i: (0, i, 0))
    dil_shape = lambda dil: jax.ShapeDtypeStruct((dil, s // dil, DIL_QKV_W), BF16)
    dils = [dil for _, dil in DIL_CONFIGS]
    return pl.pallas_call(
        _proj_kernel,
        grid=(s // tm,),
        in_specs=[rows(D_MODEL),
                  full(w_all.shape), full(wvt.shape), full(bf.shape), full(mem_b.shape), full(wkv_b.shape)],
        out_specs=[rows(W_DIL), rows(LANES),
                   pl.BlockSpec((FOX_PAIRS, VT_ROWS, tm), lambda i: (0, 0, i)),
                   rows(MEM_W)] + [dil_spec(d) for d in dils],
        out_shape=[jax.ShapeDtypeStruct((s, W_DIL), BF16),
                   jax.ShapeDtypeStruct((s, LANES), BF16),
                   jax.ShapeDtypeStruct((FOX_PAIRS, VT_ROWS, s), BF16),
                   jax.ShapeDtypeStruct((s, MEM_W), BF16)] + [dil_shape(d) for d in dils],
        scratch_shapes=[pltpu.VMEM((MEM_LEN, 2 * MEM_W), BF16), pltpu.VMEM((8, LANES), F32),
                        pltpu.VMEM((3, tm, LANES), F32)],
        compiler_params=pltpu.CompilerParams(dimension_semantics=("arbitrary",),
                                             vmem_limit_bytes=VMEM_LIMIT),
        name="proj",
    )(x2, w_all, wvt, bf, mem_b, wkv_b)


def _fox_kernel(q_ref, k_ref, e_ref, vt_ref, o_ref, m_ref, acc_ref):
    p = pl.program_id(0)
    i = pl.program_id(1)
    t = q_ref.shape[0]
    tk = min(FOX_TK, t)
    q = q_ref[...]
    lane = lax.broadcasted_iota(jnp.int32, (t, LANES), 1)
    qx = []
    for a in range(2):
        head = (lane < HEAD_DIM) if a == 0 else (lane >= HEAD_DIM)
        lo = FG_PAIR_STRIDE * p + FG_PIECES * a
        ones = jnp.where((lane >= lo) & (lane < lo + FG_PIECES), 1.0, 0.0).astype(BF16)
        qx.append(jnp.concatenate([jnp.where(head, q, jnp.zeros_like(q)), ones], axis=1))

    m_ref[...] = jnp.full(m_ref.shape, -jnp.inf, F32)
    acc_ref[...] = jnp.zeros(acc_ref.shape, F32)

    def step(start, q0):
        kx = jnp.concatenate([k_ref[pl.ds(start, tk), :], e_ref[pl.ds(start, tk), :]], axis=1)
        vtb = vt_ref[:, pl.ds(start, tk)]
        scores = [_dot_nt(kx, qx[a][q0:, :]) for a in range(2)]
        for a in range(2):
            s = scores[a]
            if q0 is not None:
                kpos = lax.broadcasted_iota(jnp.int32, s.shape, 0)
                qpos = lax.broadcasted_iota(jnp.int32, s.shape, 1)
                s = jnp.where(kpos <= qpos, s, -jnp.inf)
            qs = slice(q0, t)
            m_prev = m_ref[a, :, qs]
            m_new = jnp.maximum(m_prev, jnp.max(s, axis=0, keepdims=True))
            alpha = jnp.exp(m_prev - m_new)
            pt = jnp.exp(s - m_new).astype(BF16)
            m_ref[a, :, qs] = m_new
            acc_ref[a, :, qs] = alpha * acc_ref[a, :, qs] + _dot(vtb, pt)

    def full_step(j, _):
        step(pl.multiple_of(j * tk, tk), None)
        return 0

    lax.fori_loop(0, i * (t // tk), full_step, 0)
    for d in range(t // tk):
        step(pl.multiple_of(i * t + d * tk, tk), d * tk)
    o0 = acc_ref[0, 0:HEAD_DIM, :] / acc_ref[0, LANES:LANES + 1, :]
    o1 = acc_ref[1, HEAD_DIM:LANES, :] / acc_ref[1, LANES:LANES + 1, :]
    o_ref[...] = jnp.concatenate([o0, o1], axis=0).T.astype(BF16)


def _fox(qk, ext, vt):
    s = qk.shape[0]
    t = min(FOX_TQ, s)
    assert s % t == 0 and t % min(FOX_TK, t) == 0
    kcol = W_FOXK // LANES
    return pl.pallas_call(
        _fox_kernel,
        grid=(FOX_PAIRS, s // t),
        in_specs=[pl.BlockSpec((t, LANES), lambda p, i: (i, p)),
                  pl.BlockSpec((s, LANES), lambda p, i: (0, kcol + p)),
                  pl.BlockSpec((s, LANES), lambda p, i: (0, 0)),
                  pl.BlockSpec((None, VT_ROWS, s), lambda p, i: (p, 0, 0))],
        out_specs=pl.BlockSpec((t, LANES), lambda p, i: (i, p)),
        out_shape=jax.ShapeDtypeStruct((s, FOX_W), BF16),
        scratch_shapes=[pltpu.VMEM((2, 1, t), F32), pltpu.VMEM((2, VT_ROWS, t), F32)],
        compiler_params=pltpu.CompilerParams(dimension_semantics=("arbitrary", "arbitrary"),
                                             vmem_limit_bytes=VMEM_LIMIT),
        name="fox",
    )(qk, qk, ext, vt)


def _dil_kernel(table_ref, cur_ref, prev_ref, bucket_ref, o_ref, bias_ref, *, group, blocks_per_seq):
    j = pl.program_id(0)
    nblk = cur_ref.shape[0] // DBLK
    ii = lax.broadcasted_iota(jnp.int32, (DBLK, 2 * DBLK), 0)
    jj = lax.broadcasted_iota(jnp.int32, (DBLK, 2 * DBLK), 1)

    @pl.when(j == 0)
    def _():
        bucket = bucket_ref[...]
        rel = DBLK + ii - jj
        in_window = (rel >= 0) & (rel <= DBLK)
        for a in range(2):
            b = jnp.zeros(bucket.shape, F32)
            for t in range(T5_BUCKETS):
                b = jnp.where(bucket == t, table_ref[t * DIL_HEADS + 2 * group + a], b)
            bias_ref[a] = jnp.where(in_window, b, -jnp.inf)

    first = _lane_lt((DBLK, LANES), HEAD_DIM)
    chunk_has_prev = ((j * nblk) % blocks_per_seq) != 0
    for b in range(nblk):
        rows = slice(b * DBLK, (b + 1) * DBLK)
        q = cur_ref[rows, 0:LANES]
        if b == 0:
            kprev, vprev = prev_ref[:, LANES:2 * LANES], prev_ref[:, 2 * LANES:3 * LANES]
        else:
            before = slice((b - 1) * DBLK, b * DBLK)
            kprev, vprev = cur_ref[before, LANES:2 * LANES], cur_ref[before, 2 * LANES:3 * LANES]
        kcat = jnp.concatenate([kprev, cur_ref[rows, LANES:2 * LANES]], axis=0)
        vcat = jnp.concatenate([vprev, cur_ref[rows, 2 * LANES:3 * LANES]], axis=0)
        outs, lses = [], []
        for a in range(2):
            qa = jnp.where(first if a == 0 else jnp.logical_not(first), q, jnp.zeros_like(q))
            s = _dot_nt(qa, kcat) + bias_ref[a]
            if b == 0:
                s = jnp.where(chunk_has_prev | (jj >= DBLK), s, -jnp.inf)
            m = jnp.max(s, axis=-1, keepdims=True)
            e = jnp.exp(s - m)
            den = jnp.sum(e, axis=-1, keepdims=True)
            outs.append(_dot(e.astype(BF16), vcat) / den)
            lses.append(jnp.broadcast_to(m + jnp.log(den), (DBLK, LANES)))
        o_ref[rows, 0:LANES] = jnp.where(first, outs[0], outs[1])
        o_ref[rows, LANES:2 * LANES] = jnp.where(first, lses[0], lses[1])


def _dil(group, dil, table_flat, qkv_sub, bucket):
    s = qkv_sub.shape[0]
    seq = s // dil
    chunk = min(DIL_CHUNK, seq)
    assert seq % chunk == 0 and chunk % DBLK == 0
    nblk = chunk // DBLK
    body = functools.partial(_dil_kernel, group=group, blocks_per_seq=seq // DBLK)
    return pl.pallas_call(
        body,
        grid=(s // chunk,),
        in_specs=[pl.BlockSpec(memory_space=pltpu.SMEM),
                  pl.BlockSpec((chunk, DIL_QKV_W), lambda j: (j, 0)),
                  pl.BlockSpec((DBLK, DIL_QKV_W), lambda j: (jnp.maximum(j * nblk - 1, 0), 0)),
                  pl.BlockSpec((DBLK, 2 * DBLK), lambda j: (0, 0))],
        out_specs=pl.BlockSpec((chunk, 2 * LANES), lambda j: (j, 0)),
        out_shape=jax.ShapeDtypeStruct((s, 2 * LANES), F32),
        scratch_shapes=[pltpu.VMEM((2, DBLK, 2 * DBLK), F32)],
        compiler_params=pltpu.CompilerParams(dimension_semantics=("arbitrary",),
                                             vmem_limit_bytes=VMEM_LIMIT),
        name=f"dil{group}",
    )(table_flat, qkv_sub, qkv_sub, bucket)


def _t5_bucket(dist):
    is_small = dist < T5_MAX_EXACT
    nf = jnp.maximum(dist, T5_MAX_EXACT).astype(F32)
    large = T5_MAX_EXACT + (jnp.log(nf / T5_MAX_EXACT) / np.log(T5_MAX_DISTANCE / T5_MAX_EXACT)
                            * (T5_BUCKETS - T5_MAX_EXACT)).astype(jnp.int32)
    large = jnp.minimum(large, T5_BUCKETS - 1)
    return jnp.where(is_small, dist, large)


def _merge_kernel(x_ref, ofox_ref, od0_ref, od1_ref, od2_ref, omem_ref, wg_ref, wbf_ref, wbd_ref, wbm_ref,
                  wout_ref, g_ref, b_ref, wr_ref, br_ref, h_ref, h3_ref, ri_ref, rg_ref, cnt_ref,
                  carry_ref, tok_ref):
    i = pl.program_id(0)
    tm = x_ref.shape[0]

    @pl.when(i == 0)
    def _():
        carry_ref[...] = jnp.zeros_like(carry_ref)

    x = x_ref[...]
    xb = x.astype(BF16)

    def gate(br):
        return jax.nn.sigmoid(_dot(xb, wg_ref[:, br * D_MODEL:(br + 1) * D_MODEL]))

    merged = gate(0) * _dot(ofox_ref[...], wbf_ref[...])

    for g, (od_ref, (_, dil)) in enumerate(zip((od0_ref, od1_ref, od2_ref), DIL_CONFIGS)):
        for r in range(dil):
            for half in range(2):
                tok_ref[g, half, pl.ds(r, tm // dil, stride=dil), :] = od_ref[r, :, half * LANES:(half + 1) * LANES]
    lse = [tok_ref[g, 1] for g in range(DIL_GROUPS)]
    mx = jnp.maximum(jnp.maximum(lse[0], lse[1]), lse[2])
    ex = [jnp.exp(l - mx) for l in lse]
    den = ex[0] + ex[1] + ex[2]
    bd = None
    for g in range(DIL_GROUPS):
        og = (tok_ref[g, 0] * (ex[g] / den)).astype(BF16)
        t = _dot(og, wbd_ref[g * LANES:(g + 1) * LANES, :])
        bd = t if bd is None else bd + t
    merged = merged + gate(1) * bd
    merged = merged + gate(2) * _dot(omem_ref[...], wbm_ref[...])

    y = _dot(merged.astype(BF16), wout_ref[...])
    h = _layer_norm(DEEPNORM_ALPHA * x + y, g_ref[...], b_ref[...])
    h_ref[...] = h
    _store_token_tiles(h3_ref, h)

    lane = lax.broadcasted_iota(jnp.int32, (tm, LANES), 1)
    logits = _dot(h.astype(BF16), wr_ref[...]) + br_ref[...]
    vals = jnp.where(lane < N_EXPERTS, logits, -jnp.inf)
    tops, hots = [], []
    for _ in range(TOP_K):
        mk = jnp.max(vals, axis=-1, keepdims=True)
        ik = jnp.min(jnp.where(vals == mk, lane, LANES), axis=-1, keepdims=True)
        hot = lane == ik
        vals = jnp.where(hot, -jnp.inf, vals)
        tops.append((mk, ik))
        hots.append(hot)
    es = [jnp.exp(mk - tops[0][0]) for mk, _ in tops]
    esum = es[0] + es[1] + es[2] + es[3]

    picked = jnp.where(hots[0] | hots[1] | hots[2] | hots[3], 1.0, 0.0)
    row = lax.broadcasted_iota(jnp.int32, (tm, tm), 0)
    col = lax.broadcasted_iota(jnp.int32, (tm, tm), 1)
    lower = jnp.where(col < row, 1.0, 0.0).astype(BF16)
    before = _dot(lower, picked.astype(BF16)) + carry_ref[0:1, :]
    ri = jnp.zeros((tm, LANES), jnp.int32)
    rg = jnp.zeros((tm, LANES), F32)
    for k in range(TOP_K):
        rank = jnp.sum(jnp.where(hots[k], before, 0.0), axis=-1, keepdims=True)
        ri = jnp.where(lane == k, tops[k][1], ri)
        ri = jnp.where(lane == TOP_K + k, rank.astype(jnp.int32), ri)
        rg = jnp.where(lane == k, es[k] / esum, rg)
    ri_ref[...] = ri
    rg_ref[...] = rg
    total = carry_ref[...] + jnp.sum(picked, axis=0, keepdims=True)
    carry_ref[...] = total
    cnt_ref[...] = total


def _merge(x2, ofox, ods, omem, wg, wbf, wbd, wbm, wout, g1, b1, wr, br):
    s = x2.shape[0]
    tm = min(MERGE_TM, s)
    full = lambda a: pl.BlockSpec(a.shape, lambda i: (0,) * a.ndim)
    rows = lambda w: pl.BlockSpec((tm, w), lambda i: (i, 0))
    od_specs = [pl.BlockSpec((dil, tm // dil, 2 * LANES), lambda i: (0, i, 0)) for _, dil in DIL_CONFIGS]
    return pl.pallas_call(
        _merge_kernel,
        grid=(s // tm,),
        in_specs=[rows(D_MODEL), rows(FOX_W)] + od_specs +
                 [rows(MEM_W), full(wg), full(wbf), full(wbd), full(wbm), full(wout),
                  full(g1), full(b1), full(wr), full(br)],
        out_specs=[rows(D_MODEL), pl.BlockSpec((tm * SUBLANES, LANES), lambda i: (i, 0)),
                   rows(LANES), rows(LANES), pl.BlockSpec((8, LANES), lambda i: (0, 0))],
        out_shape=[jax.ShapeDtypeStruct((s, D_MODEL), F32),
                   jax.ShapeDtypeStruct((s * SUBLANES, LANES), F32),
                   jax.ShapeDtypeStruct((s, LANES), jnp.int32),
                   jax.ShapeDtypeStruct((s, LANES), F32),
                   jax.ShapeDtypeStruct((8, LANES), F32)],
        scratch_shapes=[pltpu.VMEM((8, LANES), F32), pltpu.VMEM((DIL_GROUPS, 2, tm, LANES), F32)],
        compiler_params=pltpu.CompilerParams(dimension_semantics=("arbitrary",),
                                             vmem_limit_bytes=VMEM_LIMIT),
        name="merge",
    )(x2, ofox, *ods, omem, wg, wbf, wbd, wbm, wout, g1, b1, wr, br)


def _row_copy(src_ref, src_row, dst_ref, dst_row, sem):
    tile = lambda ref, row: ref.at[pl.ds(pl.multiple_of(row * SUBLANES, SUBLANES), SUBLANES), :]
    return pltpu.make_async_copy(tile(src_ref, src_row), tile(dst_ref, dst_row), sem)


def _dispatch_kernel(e_ref, r_ref, ps_ref, h_ref, xb_in_ref, xb_ref, sem):
    del xb_in_ref
    tm = h_ref.shape[0] // SUBLANES

    def start(t, _):
        for k in range(TOP_K):
            a = t * TOP_K + k
            _row_copy(h_ref, t, xb_ref, ps_ref[e_ref[a]] + r_ref[a], sem).start(priority=k % 2)
        return 0

    lax.fori_loop(0, tm, start, 0, unroll=ROW_DMA_UNROLL)

    def wait(t, _):
        for k in range(TOP_K):
            _row_copy(h_ref, 0, xb_ref, 0, sem).wait()
        return 0

    lax.fori_loop(0, tm, wait, 0, unroll=ROW_DMA_UNROLL)


def _dispatch(e_flat, r_flat, pad_starts, h3, xb0):
    s = h3.shape[0] // SUBLANES
    tm = min(DISPATCH_TM, s)
    smem1 = lambda n: pl.BlockSpec((n,), lambda i: (i,), memory_space=pltpu.SMEM)
    return pl.pallas_call(
        _dispatch_kernel,
        grid=(s // tm,),
        in_specs=[smem1(tm * TOP_K), smem1(tm * TOP_K),
                  pl.BlockSpec(memory_space=pltpu.SMEM),
                  pl.BlockSpec((tm * SUBLANES, LANES), lambda i: (i, 0)),
                  pl.BlockSpec(memory_space=pl.ANY)],
        out_specs=pl.BlockSpec(memory_space=pl.ANY),
        out_shape=jax.ShapeDtypeStruct(xb0.shape, xb0.dtype),
        scratch_shapes=[pltpu.SemaphoreType.DMA(())],
        input_output_aliases={4: 0},
        compiler_params=pltpu.CompilerParams(dimension_semantics=("arbitrary",),
                                             vmem_limit_bytes=VMEM_LIMIT),
        name="dispatch",
    )(e_flat, r_flat, pad_starts, h3, xb0)


def _expert_kernel(te_ref, nu_ref, x_ref, w1_ref, b1_ref, w2_ref, b2_ref, y_ref, w1b_ref, w2b_ref):
    b = pl.program_id(0)

    @pl.when(b >= nu_ref[0])
    def _():
        y_ref[...] = jnp.zeros(y_ref.shape, y_ref.dtype)

    @pl.when(b < nu_ref[0])
    def _():
        prev = te_ref[jnp.maximum(b - 1, 0)]

        @pl.when((b == 0) | (te_ref[b] != prev))
        def _():
            chunk = 128
            def cast(r, _):
                rows = pl.ds(pl.multiple_of(r * chunk, chunk), chunk)
                w1b_ref[rows, :] = w1_ref[rows, :].astype(BF16)
                w2b_ref[rows, :] = w2_ref[rows, :].astype(BF16)
                return 0
            lax.fori_loop(0, D_MODEL // chunk, cast, 0)

        xe = _load_token_tiles(x_ref, 0, x_ref.shape[0] // SUBLANES).astype(BF16)
        hcat = _dot(xe, w1b_ref[...]) + b1_ref[...]
        h_glu = jnp.minimum(hcat[:, :D_FF], SWIGLU_LIMIT)
        h_lin = jnp.clip(hcat[:, D_FF:], -SWIGLU_LIMIT, SWIGLU_LIMIT)
        act = h_glu * jax.nn.sigmoid(SWIGLU_ALPHA * h_glu) * (h_lin + 1.0)
        _store_token_tiles(y_ref, _dot(act.astype(BF16), w2b_ref[...]) + b2_ref[...])


def _experts(tile_e, n_used, xb, w1, b1, w2, b2):
    p = xb.shape[0] // SUBLANES
    tm = EXPERT_TM
    row_map = lambda b, te, nu: (jnp.minimum(b, nu[0] - 1), 0)
    exp_map = lambda b, te, nu: (te[b], 0, 0)
    return pl.pallas_call(
        _expert_kernel,
        grid_spec=pltpu.PrefetchScalarGridSpec(
            num_scalar_prefetch=2,
            grid=(p // tm,),
            in_specs=[pl.BlockSpec((tm * SUBLANES, LANES), row_map),
                      pl.BlockSpec((None, D_MODEL, 2 * D_FF), exp_map),
                      pl.BlockSpec((None, 1, 2 * D_FF), exp_map),
                      pl.BlockSpec((None, D_FF, D_MODEL), exp_map),
                      pl.BlockSpec((None, 1, D_MODEL), exp_map)],
            out_specs=pl.BlockSpec((tm * SUBLANES, LANES), lambda b, te, nu: (b, 0)),
            scratch_shapes=[pltpu.VMEM((D_MODEL, 2 * D_FF), BF16), pltpu.VMEM((D_FF, D_MODEL), BF16)]),
        out_shape=jax.ShapeDtypeStruct((p * SUBLANES, LANES), F32),
        compiler_params=pltpu.CompilerParams(dimension_semantics=("arbitrary",),
                                             vmem_limit_bytes=VMEM_LIMIT),
        name="experts",
    )(tile_e, n_used, xb, w1, b1, w2, b2)


def _combine_kernel(e_ref, r_ref, ps_ref, h_ref, rg_ref, g_ref, b_ref, yb_ref, o_ref, buf_ref, sem):
    tm = h_ref.shape[0]

    def start(t, _):
        for k in range(TOP_K):
            a = t * TOP_K + k
            _row_copy(yb_ref, ps_ref[e_ref[a]] + r_ref[a], buf_ref, k * tm + t, sem).start(priority=k % 2)
        return 0

    lax.fori_loop(0, tm, start, 0, unroll=ROW_DMA_UNROLL)

    def wait(t, _):
        for k in range(TOP_K):
            _row_copy(yb_ref, 0, buf_ref, 0, sem).wait()
        return 0

    lax.fori_loop(0, tm, wait, 0, unroll=ROW_DMA_UNROLL)

    h = h_ref[...]
    rg = rg_ref[...]
    moe = None
    for k in range(TOP_K):
        t = rg[:, k:k + 1] * _load_token_tiles(buf_ref, k * tm, tm)
        moe = t if moe is None else moe + t
    o_ref[...] = _layer_norm(DEEPNORM_ALPHA * h + moe, g_ref[...], b_ref[...])


def _combine(e_flat, r_flat, pad_starts, h, rg, g2, b2, yb):
    s = h.shape[0]
    tm = min(COMBINE_TM, s)
    smem1 = lambda n: pl.BlockSpec((n,), lambda i: (i,), memory_space=pltpu.SMEM)
    full = lambda a: pl.BlockSpec(a.shape, lambda i: (0,) * a.ndim)
    return pl.pallas_call(
        _combine_kernel,
        grid=(s // tm,),
        in_specs=[smem1(tm * TOP_K), smem1(tm * TOP_K),
                  pl.BlockSpec(memory_space=pltpu.SMEM),
                  pl.BlockSpec((tm, D_MODEL), lambda i: (i, 0)),
                  pl.BlockSpec((tm, LANES), lambda i: (i, 0)),
                  full(g2), full(b2),
                  pl.BlockSpec(memory_space=pl.ANY)],
        out_specs=pl.BlockSpec((tm, D_MODEL), lambda i: (i, 0)),
        out_shape=jax.ShapeDtypeStruct((s, D_MODEL), F32),
        scratch_shapes=[pltpu.VMEM((TOP_K * tm * SUBLANES, LANES), F32), pltpu.SemaphoreType.DMA(())],
        compiler_params=pltpu.CompilerParams(dimension_semantics=("arbitrary",),
                                             vmem_limit_bytes=VMEM_LIMIT),
        name="combine",
    )(e_flat, r_flat, pad_starts, h, rg, g2, b2, yb)


def kernel(x, mem, w_in, b_fgate, t5_bias, w_mem_kv, w_br_fox, w_br_dil, w_br_mem, w_out, ln1_g, ln1_b,
           w_router, b_router, w_exp_in, b_exp_in, w_exp_out, b_exp_out, ln2_g, ln2_b):
    bsz, s, d = x.shape
    assert bsz == 1 and d == D_MODEL and w_in.shape[0] == 1
    assert s % (DIL_CONFIGS[-1][1] * DBLK) == 0
    x2 = x[0]
    w_in0 = w_in[0]

    fg_lanes = np.array([FG_PAIR_STRIDE * (h // 2) + FG_PIECES * (h % 2) + j
                         for h in range(FOX_HEADS) for j in range(FG_PIECES)])
    fg_heads = np.repeat(np.arange(FOX_HEADS), FG_PIECES)
    w_fg = jnp.zeros((D_MODEL, LANES), F32).at[:, fg_lanes].set(w_in0[:, OFF_FOX_F + fg_heads])
    bf = jnp.zeros((1, LANES), F32).at[0, fg_lanes].set(b_fgate[0][fg_heads])
    w_all = jnp.concatenate([w_in0[:, OFF_FOX_QKV:OFF_FOX_QKV + 2 * FOX_W], w_in0[:, OFF_DIL_QKV:OFF_GATES],
                             w_fg], axis=1).astype(BF16)
    wvt = w_in0[:, OFF_FOX_QKV + 2 * FOX_W:OFF_FOX_F].T.astype(BF16)
    w_gates = w_in0[:, OFF_GATES:].astype(BF16)

    qk, ext, vt, o_mem, *dil_qkv = _proj(x2, w_all, wvt, bf, mem[0].astype(BF16), w_mem_kv[0].astype(BF16))

    o_fox = _fox(qk, ext, vt)

    ii = np.arange(DBLK, dtype=np.int32)[:, None]
    jj = np.arange(2 * DBLK, dtype=np.int32)[None, :]
    rel = np.clip(DBLK + ii - jj, 0, None)
    table_flat = t5_bias.reshape(-1)
    ods = []
    for g, (window, dil) in enumerate(DIL_CONFIGS):
        assert window // dil == DBLK
        bucket = _t5_bucket(jnp.asarray(rel * dil, dtype=jnp.int32))
        od = _dil(g, dil, table_flat, dil_qkv[g].reshape(s, DIL_QKV_W), bucket)
        ods.append(od.reshape(dil, s // dil, 2 * LANES))

    wr = jnp.zeros((D_MODEL, LANES), F32).at[:, :N_EXPERTS].set(w_router[0]).astype(BF16)
    br = jnp.zeros((1, LANES), F32).at[0, :N_EXPERTS].set(b_router[0])
    h1, h3, ri, rg, cnt = _merge(x2, o_fox, ods, o_mem, w_gates, w_br_fox[0].astype(BF16),
                                 w_br_dil[0].astype(BF16), w_br_mem[0].astype(BF16), w_out[0].astype(BF16),
                                 ln1_g, ln1_b, wr, br)

    tm = EXPERT_TM
    counts = cnt[0, :N_EXPERTS].astype(jnp.int32)
    padded = (counts + tm - 1) // tm * tm
    pad_ends = jnp.cumsum(padded)
    pad_starts = (pad_ends - padded).astype(jnp.int32)
    n_tiles = (s * TOP_K) // tm + N_EXPERTS
    n_used = (pad_ends[-1] // tm).astype(jnp.int32)
    tile_lo = jnp.arange(n_tiles, dtype=jnp.int32) * tm
    tile_e = jnp.minimum(jnp.sum((pad_ends[None, :] <= tile_lo[:, None]).astype(jnp.int32), axis=1),
                         N_EXPERTS - 1)
    tile_e = tile_e[jnp.minimum(jnp.arange(n_tiles), n_used - 1)]
    e_flat = ri[:, 0:TOP_K].reshape(-1)
    r_flat = ri[:, TOP_K:2 * TOP_K].reshape(-1)

    xb = _dispatch(e_flat, r_flat, pad_starts, h3, jnp.zeros((n_tiles * tm * SUBLANES, LANES), F32))
    yb = _experts(tile_e, n_used.reshape(1), xb, w_exp_in[0], b_exp_in[0][:, None, :], w_exp_out[0],
                  b_exp_out[0][:, None, :])
    out = _combine(e_flat, r_flat, pad_starts, h1, rg, ln2_g, ln2_b, yb)
    return out[None]
```

```python
import functools

import numpy as np
import jax
import jax.numpy as jnp
from jax import lax
from jax.experimental import pallas as pl
from jax.experimental.pallas import tpu as pltpu

D_MODEL = 1024
HEAD_DIM = 64
LANES = 128
SUBLANES = 8
assert D_MODEL == SUBLANES * LANES
FOX_HEADS = 6
FOX_PAIRS = FOX_HEADS // 2
DIL_CONFIGS = ((128, 1), (512, 4), (2048, 16))
DIL_GROUPS = len(DIL_CONFIGS)
DIL_HEADS = 2 * DIL_GROUPS
MEM_HEADS = 4
MEM_PAIRS = MEM_HEADS // 2
MEM_LEN = 256
FOX_W = FOX_HEADS * HEAD_DIM
DIL_W = DIL_HEADS * HEAD_DIM
MEM_W = MEM_HEADS * HEAD_DIM
DBLK = 128
DIL_QKV_W = 3 * LANES
T5_BUCKETS = 32
T5_MAX_EXACT = T5_BUCKETS // 2
T5_MAX_DISTANCE = 2048
N_EXPERTS = 32
TOP_K = 4
D_FF = D_MODEL
SWIGLU_LIMIT = 7.0
SWIGLU_ALPHA = 1.702
LN_EPS = 1e-5
DEEPNORM_ALPHA = 2.0 ** 0.25
QK_SCALE = HEAD_DIM ** -0.5

OFF_FOX_QKV = 0
OFF_FOX_F = OFF_FOX_QKV + 3 * FOX_W
OFF_DIL_QKV = OFF_FOX_F + FOX_HEADS
OFF_MEM_Q = OFF_DIL_QKV + 3 * DIL_W
OFF_GATES = OFF_MEM_Q + MEM_W

W_FOXQ = 0
W_FOXK = FOX_W
W_DIL = 2 * FOX_W
W_MEMQ = W_DIL + 3 * DIL_W
W_FGATE = W_MEMQ + MEM_W
W_ALL = W_FGATE + LANES
FG_PAIR_STRIDE = 8
FG_PIECES = 3
VT_ROWS = HEAD_DIM + 16

VMEM_LIMIT = 56 * 1024 * 1024

PROJ_TM = 512
FOX_TQ = 1024
FOX_TK = 512
DIL_CHUNK = 512
MERGE_TM = 512
DISPATCH_TM = 256
EXPERT_TM = 256
COMBINE_TM = 256
ROW_DMA_UNROLL = 4

BF16 = jnp.bfloat16
F32 = jnp.float32


def _dot(a, b):
    return jnp.dot(a, b, preferred_element_type=F32)


def _dot_nt(a, b):
    return lax.dot_general(a, b, (((1,), (1,)), ((), ())), preferred_element_type=F32)


def _lane_lt(shape, n):
    return lax.broadcasted_iota(jnp.int32, shape, len(shape) - 1) < n


def _split3(v):
    hi = v.astype(BF16)
    r1 = v - hi.astype(F32)
    mid = r1.astype(BF16)
    lo = (r1 - mid.astype(F32)).astype(BF16)
    return hi, mid, lo


def _store_token_tiles(ref, v):
    n = v.shape[0]
    for c in range(SUBLANES):
        ref[pl.ds(c, n, stride=SUBLANES), :] = v[:, c * LANES:(c + 1) * LANES]


def _load_token_tiles(ref, lo, n):
    return jnp.concatenate([ref[pl.ds(lo * SUBLANES + c, n, stride=SUBLANES), :] for c in range(SUBLANES)],
                           axis=1)


def _layer_norm(r, g, b):
    mu = jnp.mean(r, axis=-1, keepdims=True)
    d = r - mu
    var = jnp.mean(d * d, axis=-1, keepdims=True)
    return d * lax.rsqrt(var + LN_EPS) * g + b


def _proj_kernel(x_ref, w_ref, wvt_ref, bf_ref, mem_ref, wkv_ref, qk_ref, ext_ref, vt_ref, omem_ref,
                 d0_ref, d1_ref, d2_ref, kvm_ref, carry_ref, stage_ref):
    i = pl.program_id(0)
    tm = x_ref.shape[0]

    @pl.when(i == 0)
    def _():
        kvm_ref[...] = _dot(mem_ref[...], wkv_ref[...]).astype(BF16)
        carry_ref[...] = jnp.zeros_like(carry_ref)

    xb = x_ref[...].astype(BF16)
    proj = _dot(xb, w_ref[...])

    qk_ref[:, W_FOXQ:W_FOXK] = (proj[:, W_FOXQ:W_FOXK] * QK_SCALE).astype(BF16)
    qk_ref[:, W_FOXK:W_DIL] = proj[:, W_FOXK:W_DIL].astype(BF16)

    for g, (d_ref, (_, dil)) in enumerate(zip((d0_ref, d1_ref, d2_ref), DIL_CONFIGS)):
        cols = [proj[:, W_DIL + part * DIL_W + g * LANES:W_DIL + part * DIL_W + (g + 1) * LANES]
                for part in range(3)]
        cols[0] = cols[0] * QK_SCALE
        if dil == 1:
            for part in range(3):
                d_ref[0, :, part * LANES:(part + 1) * LANES] = cols[part].astype(BF16)
        else:
            for part in range(3):
                stage_ref[part] = cols[part]
            for r in range(dil):
                for part in range(3):
                    d_ref[r, :, part * LANES:(part + 1) * LANES] = (
                        stage_ref[part, pl.ds(r, tm // dil, stride=dil), :].astype(BF16))

    vt = _dot_nt(wvt_ref[...], xb)
    for h in range(FOX_HEADS):
        vt_ref[h, 0:HEAD_DIM, :] = vt[h * HEAD_DIM:(h + 1) * HEAD_DIM, :].astype(BF16)
        vt_ref[h, HEAD_DIM:VT_ROWS, :] = jnp.ones((VT_ROWS - HEAD_DIM, tm), BF16)

    z = proj[:, W_FGATE:W_ALL] + bf_ref[...]
    logf = jnp.minimum(z, 0.0) - jnp.log1p(jnp.exp(-jnp.abs(z)))
    row = lax.broadcasted_iota(jnp.int32, (tm, tm), 0)
    col = lax.broadcasted_iota(jnp.int32, (tm, tm), 1)
    lower = jnp.where(col <= row, 1.0, 0.0).astype(BF16)
    hi, mid, lo = _split3(logf)
    c = (_dot(lower, lo) + _dot(lower, mid)) + _dot(lower, hi) + carry_ref[0:1, :]
    carry_ref[...] = jnp.broadcast_to(c[tm - 1:tm, :], carry_ref.shape)
    nhi, nmid, nlo = _split3(-c)
    lane = lax.broadcasted_iota(jnp.int32, (tm, LANES), 1)
    slot = lane & (FG_PAIR_STRIDE - 1)
    piece = jnp.where(slot >= FG_PIECES, slot - FG_PIECES, slot)
    used = (slot < 2 * FG_PIECES) & (lane < FOX_PAIRS * FG_PAIR_STRIDE)
    ext = jnp.where(piece == 0, nhi, jnp.where(piece == 1, nmid, nlo))
    ext_ref[...] = jnp.where(used, ext, jnp.zeros_like(ext))

    qm = (proj[:, W_MEMQ:W_FGATE] * QK_SCALE).astype(BF16)
    for p in range(MEM_PAIRS):
        qp = qm[:, p * LANES:(p + 1) * LANES]
        kp = kvm_ref[:, p * LANES:(p + 1) * LANES]
        vp = kvm_ref[:, MEM_W + p * LANES:MEM_W + (p + 1) * LANES]
        first = _lane_lt(qp.shape, HEAD_DIM)
        outs = []
        for a in range(2):
            qa = jnp.where(first if a == 0 else jnp.logical_not(first), qp, jnp.zeros_like(qp))
            s = _dot_nt(qa, kp)
            m = jnp.max(s, axis=-1, keepdims=True)
            e = jnp.exp(s - m)
            den = jnp.sum(e, axis=-1, keepdims=True)
            outs.append(_dot(e.astype(BF16), vp) / den)
        omem_ref[:, p * LANES:(p + 1) * LANES] = jnp.where(first, outs[0], outs[1]).astype(BF16)


def _proj(x2, w_all, wvt, bf, mem_b, wkv_b):
    s = x2.shape[0]
    tm = min(PROJ_TM, s)
    full = lambda shape: pl.BlockSpec(shape, lambda i: (0,) * len(shape))
    rows = lambda w: pl.BlockSpec((tm, w), lambda i: (i, 0))
    dil_spec = lambda dil: pl.BlockSpec((dil, tm // dil, DIL_QKV_W), lambda i: (0, i, 0))
    dil_shape = lambda dil: jax.ShapeDtypeStruct((dil, s // dil, DIL_QKV_W), BF16)
    dils = [dil for _, dil in DIL_CONFIGS]
    return pl.pallas_call(
        _proj_kernel,
        grid=(s // tm,),
        in_specs=[rows(D_MODEL),
                  full(w_all.shape), full(wvt.shape), full(bf.shape), full(mem_b.shape), full(wkv_b.shape)],
        out_specs=[rows(W_DIL), rows(LANES),
                   pl.BlockSpec((FOX_HEADS, VT_ROWS, tm), lambda i: (0, 0, i)),
                   rows(MEM_W)] + [dil_spec(d) for d in dils],
        out_shape=[jax.ShapeDtypeStruct((s, W_DIL), BF16),
                   jax.ShapeDtypeStruct((s, LANES), BF16),
                   jax.ShapeDtypeStruct((FOX_HEADS, VT_ROWS, s), BF16),
                   jax.ShapeDtypeStruct((s, MEM_W), BF16)] + [dil_shape(d) for d in dils],
        scratch_shapes=[pltpu.VMEM((MEM_LEN, 2 * MEM_W), BF16), pltpu.VMEM((8, LANES), F32),
                        pltpu.VMEM((3, tm, LANES), F32)],
        compiler_params=pltpu.CompilerParams(dimension_semantics=("arbitrary",),
                                             vmem_limit_bytes=VMEM_LIMIT),
        name="proj",
    )(x2, w_all, wvt, bf, mem_b, wkv_b)


def _fox_kernel(q_ref, k_ref, e_ref, vt_ref, o_ref, m_ref, acc_ref, qx_ref, s_ref):
    p = pl.program_id(0)
    i = pl.program_id(1)
    t = q_ref.shape[0]
    tk = min(FOX_TK, t)
    q = q_ref[...]
    lane = lax.broadcasted_iota(jnp.int32, (t, LANES), 1)
    for a in range(2):
        head = (lane < HEAD_DIM) if a == 0 else (lane >= HEAD_DIM)
        lo = FG_PAIR_STRIDE * p + FG_PIECES * a
        ones = jnp.where((lane >= lo) & (lane < lo + FG_PIECES), 1.0, 0.0).astype(BF16)
        qx_ref[a, :, 0:LANES] = jnp.where(head, q, jnp.zeros_like(q))
        qx_ref[a, :, LANES:2 * LANES] = ones

    m_ref[...] = jnp.full(m_ref.shape, -jnp.inf, F32)
    acc_ref[...] = jnp.zeros(acc_ref.shape, F32)

    def scores(blk, q0, buf):
        start = pl.multiple_of(blk * tk, tk)
        kx = jnp.concatenate([k_ref[pl.ds(start, tk), :], e_ref[pl.ds(start, tk), :]], axis=1)
        for a in range(2):
            s_ref[buf, a, :, q0:t] = _dot_nt(kx, qx_ref[a, q0:t, :])

    def consume(blk, q0, buf, diagonal):
        start = pl.multiple_of(blk * tk, tk)
        for a in range(2):
            vtb = vt_ref[a, :, pl.ds(start, tk)]
            s = s_ref[buf, a, :, q0:t]
            if diagonal:
                kpos = lax.broadcasted_iota(jnp.int32, s.shape, 0)
                qpos = lax.broadcasted_iota(jnp.int32, s.shape, 1)
                s = jnp.where(kpos <= qpos, s, -jnp.inf)
            m_prev = m_ref[a, :, q0:t]
            m_new = jnp.maximum(m_prev, jnp.max(s, axis=0, keepdims=True))
            alpha = jnp.exp(m_prev - m_new)
            pt = jnp.exp(s - m_new).astype(BF16)
            m_ref[a, :, q0:t] = m_new
            acc_ref[a, :, q0:t] = alpha * acc_ref[a, :, q0:t] + _dot(vtb, pt)

    nd = t // tk
    nfull = i * nd
    scores(0, 0, 0)

    def pair(j, _):
        scores(2 * j + 1, 0, 1)
        consume(2 * j, 0, 0, False)
        scores(2 * j + 2, 0, 0)
        consume(2 * j + 1, 0, 1, False)
        return 0

    lax.fori_loop(0, nfull // 2, pair, 0)
    for d in range(nd):
        if d + 1 < nd:
            scores(nfull + d + 1, (d + 1) * tk, (d + 1) % 2)
        consume(nfull + d, d * tk, d % 2, True)
    o0 = acc_ref[0, 0:HEAD_DIM, :] / acc_ref[0, HEAD_DIM:HEAD_DIM + 1, :]
    o1 = acc_ref[1, 0:HEAD_DIM, :] / acc_ref[1, HEAD_DIM:HEAD_DIM + 1, :]
    o_ref[...] = jnp.concatenate([o0, o1], axis=0).T.astype(BF16)


def _fox(qk, ext, vt):
    s = qk.shape[0]
    t = min(FOX_TQ, s)
    tk = min(FOX_TK, t)
    assert s % t == 0 and t % (2 * tk) == 0
    kcol = W_FOXK // LANES
    return pl.pallas_call(
        _fox_kernel,
        grid=(FOX_PAIRS, s // t),
        in_specs=[pl.BlockSpec((t, LANES), lambda p, i: (i, p)),
                  pl.BlockSpec((s, LANES), lambda p, i: (0, kcol + p)),
                  pl.BlockSpec((s, LANES), lambda p, i: (0, 0)),
                  pl.BlockSpec((2, VT_ROWS, s), lambda p, i: (p, 0, 0))],
        out_specs=pl.BlockSpec((t, LANES), lambda p, i: (i, p)),
        out_shape=jax.ShapeDtypeStruct((s, FOX_W), BF16),
        scratch_shapes=[pltpu.VMEM((2, 1, t), F32), pltpu.VMEM((2, VT_ROWS, t), F32),
                        pltpu.VMEM((2, t, 2 * LANES), BF16), pltpu.VMEM((2, 2, tk, t), F32)],
        compiler_params=pltpu.CompilerParams(dimension_semantics=("arbitrary", "arbitrary"),
                                             vmem_limit_bytes=VMEM_LIMIT),
        name="fox",
    )(qk, qk, ext, vt)


def _dil_kernel(table_ref, cur_ref, prev_ref, bucket_ref, o_ref, bias_ref, *, group, blocks_per_seq):
    j = pl.program_id(0)
    nblk = cur_ref.shape[0] // DBLK
    ii = lax.broadcasted_iota(jnp.int32, (DBLK, 2 * DBLK), 0)
    jj = lax.broadcasted_iota(jnp.int32, (DBLK, 2 * DBLK), 1)

    @pl.when(j == 0)
    def _():
        bucket = bucket_ref[...]
        rel = DBLK + ii - jj
        in_window = (rel >= 0) & (rel <= DBLK)
        for a in range(2):
            b = jnp.zeros(bucket.shape, F32)
            for t in range(T5_BUCKETS):
                b = jnp.where(bucket == t, table_ref[t * DIL_HEADS + 2 * group + a], b)
            bias_ref[a] = jnp.where(in_window, b, -jnp.inf)

    first = _lane_lt((DBLK, LANES), HEAD_DIM)
    chunk_has_prev = ((j * nblk) % blocks_per_seq) != 0
    for b in range(nblk):
        rows = slice(b * DBLK, (b + 1) * DBLK)
        q = cur_ref[rows, 0:LANES]
        if b == 0:
            kprev, vprev = prev_ref[:, LANES:2 * LANES], prev_ref[:, 2 * LANES:3 * LANES]
        else:
            before = slice((b - 1) * DBLK, b * DBLK)
            kprev, vprev = cur_ref[before, LANES:2 * LANES], cur_ref[before, 2 * LANES:3 * LANES]
        kcat = jnp.concatenate([kprev, cur_ref[rows, LANES:2 * LANES]], axis=0)
        vcat = jnp.concatenate([vprev, cur_ref[rows, 2 * LANES:3 * LANES]], axis=0)
        outs, lses = [], []
        for a in range(2):
            qa = jnp.where(first if a == 0 else jnp.logical_not(first), q, jnp.zeros_like(q))
            s = _dot_nt(qa, kcat) + bias_ref[a]
            if b == 0:
                s = jnp.where(chunk_has_prev | (jj >= DBLK), s, -jnp.inf)
            m = jnp.max(s, axis=-1, keepdims=True)
            e = jnp.exp(s - m)
            den = jnp.sum(e, axis=-1, keepdims=True)
            outs.append(_dot(e.astype(BF16), vcat) / den)
            lses.append(jnp.broadcast_to(m + jnp.log(den), (DBLK, LANES)))
        o_ref[rows, 0:LANES] = jnp.where(first, outs[0], outs[1])
        o_ref[rows, LANES:2 * LANES] = jnp.where(first, lses[0], lses[1])


def _dil(group, dil, table_flat, qkv_sub, bucket):
    s = qkv_sub.shape[0]
    seq = s // dil
    chunk = min(DIL_CHUNK, seq)
    assert seq % chunk == 0 and chunk % DBLK == 0
    nblk = chunk // DBLK
    body = functools.partial(_dil_kernel, group=group, blocks_per_seq=seq // DBLK)
    return pl.pallas_call(
        body,
        grid=(s // chunk,),
        in_specs=[pl.BlockSpec(memory_space=pltpu.SMEM),
                  pl.BlockSpec((chunk, DIL_QKV_W), lambda j: (j, 0)),
                  pl.BlockSpec((DBLK, DIL_QKV_W), lambda j: (jnp.maximum(j * nblk - 1, 0), 0)),
                  pl.BlockSpec((DBLK, 2 * DBLK), lambda j: (0, 0))],
        out_specs=pl.BlockSpec((chunk, 2 * LANES), lambda j: (j, 0)),
        out_shape=jax.ShapeDtypeStruct((s, 2 * LANES), F32),
        scratch_shapes=[pltpu.VMEM((2, DBLK, 2 * DBLK), F32)],
        compiler_params=pltpu.CompilerParams(dimension_semantics=("arbitrary",),
                                             vmem_limit_bytes=VMEM_LIMIT),
        name=f"dil{group}",
    )(table_flat, qkv_sub, qkv_sub, bucket)


def _t5_bucket(dist):
    is_small = dist < T5_MAX_EXACT
    nf = jnp.maximum(dist, T5_MAX_EXACT).astype(F32)
    large = T5_MAX_EXACT + (jnp.log(nf / T5_MAX_EXACT) / np.log(T5_MAX_DISTANCE / T5_MAX_EXACT)
                            * (T5_BUCKETS - T5_MAX_EXACT)).astype(jnp.int32)
    large = jnp.minimum(large, T5_BUCKETS - 1)
    return jnp.where(is_small, dist, large)


def _merge_kernel(x_ref, ofox_ref, od0_ref, od1_ref, od2_ref, omem_ref, wg_ref, wbf_ref, wbd_ref, wbm_ref,
                  wout_ref, g_ref, b_ref, wr_ref, br_ref, h_ref, h3_ref, ri_ref, rg_ref, cnt_ref,
                  carry_ref, tok_ref):
    i = pl.program_id(0)
    tm = x_ref.shape[0]

    @pl.when(i == 0)
    def _():
        carry_ref[...] = jnp.zeros_like(carry_ref)

    x = x_ref[...]
    xb = x.astype(BF16)

    def gate(br):
        return jax.nn.sigmoid(_dot(xb, wg_ref[:, br * D_MODEL:(br + 1) * D_MODEL]))

    merged = gate(0) * _dot(ofox_ref[...], wbf_ref[...])

    for g, (od_ref, (_, dil)) in enumerate(zip((od0_ref, od1_ref, od2_ref), DIL_CONFIGS)):
        for r in range(dil):
            for half in range(2):
                tok_ref[g, half, pl.ds(r, tm // dil, stride=dil), :] = od_ref[r, :, half * LANES:(half + 1) * LANES]
    lse = [tok_ref[g, 1] for g in range(DIL_GROUPS)]
    mx = jnp.maximum(jnp.maximum(lse[0], lse[1]), lse[2])
    ex = [jnp.exp(l - mx) for l in lse]
    den = ex[0] + ex[1] + ex[2]
    bd = None
    for g in range(DIL_GROUPS):
        og = (tok_ref[g, 0] * (ex[g] / den)).astype(BF16)
        t = _dot(og, wbd_ref[g * LANES:(g + 1) * LANES, :])
        bd = t if bd is None else bd + t
    merged = merged + gate(1) * bd
    merged = merged + gate(2) * _dot(omem_ref[...], wbm_ref[...])

    y = _dot(merged.astype(BF16), wout_ref[...])
    h = _layer_norm(DEEPNORM_ALPHA * x + y, g_ref[...], b_ref[...])
    h_ref[...] = h
    _store_token_tiles(h3_ref, h)

    lane = lax.broadcasted_iota(jnp.int32, (tm, LANES), 1)
    logits = _dot(h.astype(BF16), wr_ref[...]) + br_ref[...]
    vals = jnp.where(lane < N_EXPERTS, logits, -jnp.inf)
    tops, hots = [], []
    for _ in range(TOP_K):
        mk = jnp.max(vals, axis=-1, keepdims=True)
        ik = jnp.min(jnp.where(vals == mk, lane, LANES), axis=-1, keepdims=True)
        hot = lane == ik
        vals = jnp.where(hot, -jnp.inf, vals)
        tops.append((mk, ik))
        hots.append(hot)
    es = [jnp.exp(mk - tops[0][0]) for mk, _ in tops]
    esum = es[0] + es[1] + es[2] + es[3]

    picked = jnp.where(hots[0] | hots[1] | hots[2] | hots[3], 1.0, 0.0)
    row = lax.broadcasted_iota(jnp.int32, (tm, tm), 0)
    col = lax.broadcasted_iota(jnp.int32, (tm, tm), 1)
    lower = jnp.where(col < row, 1.0, 0.0).astype(BF16)
    before = _dot(lower, picked.astype(BF16)) + carry_ref[0:1, :]
    ri = jnp.zeros((tm, LANES), jnp.int32)
    rg = jnp.zeros((tm, LANES), F32)
    for k in range(TOP_K):
        rank = jnp.sum(jnp.where(hots[k], before, 0.0), axis=-1, keepdims=True)
        ri = jnp.where(lane == k, tops[k][1], ri)
        ri = jnp.where(lane == TOP_K + k, rank.astype(jnp.int32), ri)
        rg = jnp.where(lane == k, es[k] / esum, rg)
    ri_ref[...] = ri
    rg_ref[...] = rg
    total = carry_ref[...] + jnp.sum(picked, axis=0, keepdims=True)
    carry_ref[...] = total
    cnt_ref[...] = total


def _merge(x2, ofox, ods, omem, wg, wbf, wbd, wbm, wout, g1, b1, wr, br):
    s = x2.shape[0]
    tm = min(MERGE_TM, s)
    full = lambda a: pl.BlockSpec(a.shape, lambda i: (0,) * a.ndim)
    rows = lambda w: pl.BlockSpec((tm, w), lambda i: (i, 0))
    od_specs = [pl.BlockSpec((dil, tm // dil, 2 * LANES), lambda i: (0, i, 0)) for _, dil in DIL_CONFIGS]
    return pl.pallas_call(
        _merge_kernel,
        grid=(s // tm,),
        in_specs=[rows(D_MODEL), rows(FOX_W)] + od_specs +
                 [rows(MEM_W), full(wg), full(wbf), full(wbd), full(wbm), full(wout),
                  full(g1), full(b1), full(wr), full(br)],
        out_specs=[rows(D_MODEL), pl.BlockSpec((tm * SUBLANES, LANES), lambda i: (i, 0)),
                   rows(LANES), rows(LANES), pl.BlockSpec((8, LANES), lambda i: (0, 0))],
        out_shape=[jax.ShapeDtypeStruct((s, D_MODEL), F32),
                   jax.ShapeDtypeStruct((s * SUBLANES, LANES), F32),
                   jax.ShapeDtypeStruct((s, LANES), jnp.int32),
                   jax.ShapeDtypeStruct((s, LANES), F32),
                   jax.ShapeDtypeStruct((8, LANES), F32)],
        scratch_shapes=[pltpu.VMEM((8, LANES), F32), pltpu.VMEM((DIL_GROUPS, 2, tm, LANES), F32)],
        compiler_params=pltpu.CompilerParams(dimension_semantics=("arbitrary",),
                                             vmem_limit_bytes=VMEM_LIMIT),
        name="merge",
    )(x2, ofox, *ods, omem, wg, wbf, wbd, wbm, wout, g1, b1, wr, br)


def _row_copy(src_ref, src_row, dst_ref, dst_row, sem):
    tile = lambda ref, row: ref.at[pl.ds(pl.multiple_of(row * SUBLANES, SUBLANES), SUBLANES), :]
    return pltpu.make_async_copy(tile(src_ref, src_row), tile(dst_ref, dst_row), sem)


def _dispatch_kernel(e_ref, r_ref, ps_ref, h_ref, xb_in_ref, xb_ref, sem):
    del xb_in_ref
    tm = h_ref.shape[0] // SUBLANES

    def start(t, _):
        for k in range(TOP_K):
            a = t * TOP_K + k
            _row_copy(h_ref, t, xb_ref, ps_ref[e_ref[a]] + r_ref[a], sem).start(priority=k % 2)
        return 0

    lax.fori_loop(0, tm, start, 0, unroll=ROW_DMA_UNROLL)

    def wait(t, _):
        for k in range(TOP_K):
            _row_copy(h_ref, 0, xb_ref, 0, sem).wait()
        return 0

    lax.fori_loop(0, tm, wait, 0, unroll=ROW_DMA_UNROLL)


def _dispatch(e_flat, r_flat, pad_starts, h3, xb0):
    s = h3.shape[0] // SUBLANES
    tm = min(DISPATCH_TM, s)
    smem1 = lambda n: pl.BlockSpec((n,), lambda i: (i,), memory_space=pltpu.SMEM)
    return pl.pallas_call(
        _dispatch_kernel,
        grid=(s // tm,),
        in_specs=[smem1(tm * TOP_K), smem1(tm * TOP_K),
                  pl.BlockSpec(memory_space=pltpu.SMEM),
                  pl.BlockSpec((tm * SUBLANES, LANES), lambda i: (i, 0)),
                  pl.BlockSpec(memory_space=pl.ANY)],
        out_specs=pl.BlockSpec(memory_space=pl.ANY),
        out_shape=jax.ShapeDtypeStruct(xb0.shape, xb0.dtype),
        scratch_shapes=[pltpu.SemaphoreType.DMA(())],
        input_output_aliases={4: 0},
        compiler_params=pltpu.CompilerParams(dimension_semantics=("arbitrary",),
                                             vmem_limit_bytes=VMEM_LIMIT),
        name="dispatch",
    )(e_flat, r_flat, pad_starts, h3, xb0)


def _expert_kernel(te_ref, nu_ref, x_ref, w1_ref, b1_ref, w2_ref, b2_ref, y_ref, w1b_ref, w2b_ref):
    b = pl.program_id(0)

    @pl.when(b >= nu_ref[0])
    def _():
        y_ref[...] = jnp.zeros(y_ref.shape, y_ref.dtype)

    @pl.when(b < nu_ref[0])
    def _():
        prev = te_ref[jnp.maximum(b - 1, 0)]

        @pl.when((b == 0) | (te_ref[b] != prev))
        def _():
            chunk = 128
            def cast(r, _):
                rows = pl.ds(pl.multiple_of(r * chunk, chunk), chunk)
                w1b_ref[rows, :] = w1_ref[rows, :].astype(BF16)
                w2b_ref[rows, :] = w2_ref[rows, :].astype(BF16)
                return 0
            lax.fori_loop(0, D_MODEL // chunk, cast, 0)

        xe = _load_token_tiles(x_ref, 0, x_ref.shape[0] // SUBLANES).astype(BF16)
        hcat = _dot(xe, w1b_ref[...]) + b1_ref[...]
        h_glu = jnp.minimum(hcat[:, :D_FF], SWIGLU_LIMIT)
        h_lin = jnp.clip(hcat[:, D_FF:], -SWIGLU_LIMIT, SWIGLU_LIMIT)
        act = h_glu * jax.nn.sigmoid(SWIGLU_ALPHA * h_glu) * (h_lin + 1.0)
        _store_token_tiles(y_ref, _dot(act.astype(BF16), w2b_ref[...]) + b2_ref[...])


def _experts(tile_e, n_used, xb, w1, b1, w2, b2):
    p = xb.shape[0] // SUBLANES
    tm = EXPERT_TM
    row_map = lambda b, te, nu: (jnp.minimum(b, nu[0] - 1), 0)
    exp_map = lambda b, te, nu: (te[b], 0, 0)
    return pl.pallas_call(
        _expert_kernel,
        grid_spec=pltpu.PrefetchScalarGridSpec(
            num_scalar_prefetch=2,
            grid=(p // tm,),
            in_specs=[pl.BlockSpec((tm * SUBLANES, LANES), row_map),
                      pl.BlockSpec((None, D_MODEL, 2 * D_FF), exp_map),
                      pl.BlockSpec((None, 1, 2 * D_FF), exp_map),
                      pl.BlockSpec((None, D_FF, D_MODEL), exp_map),
                      pl.BlockSpec((None, 1, D_MODEL), exp_map)],
            out_specs=pl.BlockSpec((tm * SUBLANES, LANES), lambda b, te, nu: (b, 0)),
            scratch_shapes=[pltpu.VMEM((D_MODEL, 2 * D_FF), BF16), pltpu.VMEM((D_FF, D_MODEL), BF16)]),
        out_shape=jax.ShapeDtypeStruct((p * SUBLANES, LANES), F32),
        compiler_params=pltpu.CompilerParams(dimension_semantics=("arbitrary",),
                                             vmem_limit_bytes=VMEM_LIMIT),
        name="experts",
    )(tile_e, n_used, xb, w1, b1, w2, b2)


def _combine_kernel(e_ref, r_ref, ps_ref, h_ref, rg_ref, g_ref, b_ref, yb_ref, o_ref, buf_ref, sem):
    tm = h_ref.shape[0]

    def start(t, _):
        for k in range(TOP_K):
            a = t * TOP_K + k
            _row_copy(yb_ref, ps_ref[e_ref[a]] + r_ref[a], buf_ref, k * tm + t, sem).start(priority=k % 2)
        return 0

    lax.fori_loop(0, tm, start, 0, unroll=ROW_DMA_UNROLL)

    def wait(t, _):
        for k in range(TOP_K):
            _row_copy(yb_ref, 0, buf_ref, 0, sem).wait()
        return 0

    lax.fori_loop(0, tm, wait, 0, unroll=ROW_DMA_UNROLL)

    h = h_ref[...]
    rg = rg_ref[...]
    moe = None
    for k in range(TOP_K):
        t = rg[:, k:k + 1] * _load_token_tiles(buf_ref, k * tm, tm)
        moe = t if moe is None else moe + t
    o_ref[...] = _layer_norm(DEEPNORM_ALPHA * h + moe, g_ref[...], b_ref[...])


def _combine(e_flat, r_flat, pad_starts, h, rg, g2, b2, yb):
    s = h.shape[0]
    tm = min(COMBINE_TM, s)
    smem1 = lambda n: pl.BlockSpec((n,), lambda i: (i,), memory_space=pltpu.SMEM)
    full = lambda a: pl.BlockSpec(a.shape, lambda i: (0,) * a.ndim)
    return pl.pallas_call(
        _combine_kernel,
        grid=(s // tm,),
        in_specs=[smem1(tm * TOP_K), smem1(tm * TOP_K),
                  pl.BlockSpec(memory_space=pltpu.SMEM),
                  pl.BlockSpec((tm, D_MODEL), lambda i: (i, 0)),
                  pl.BlockSpec((tm, LANES), lambda i: (i, 0)),
                  full(g2), full(b2),
                  pl.BlockSpec(memory_space=pl.ANY)],
        out_specs=pl.BlockSpec((tm, D_MODEL), lambda i: (i, 0)),
        out_shape=jax.ShapeDtypeStruct((s, D_MODEL), F32),
        scratch_shapes=[pltpu.VMEM((TOP_K * tm * SUBLANES, LANES), F32), pltpu.SemaphoreType.DMA(())],
        compiler_params=pltpu.CompilerParams(dimension_semantics=("arbitrary",),
                                             vmem_limit_bytes=VMEM_LIMIT),
        name="combine",
    )(e_flat, r_flat, pad_starts, h, rg, g2, b2, yb)


def kernel(x, mem, w_in, b_fgate, t5_bias, w_mem_kv, w_br_fox, w_br_dil, w_br_mem, w_out, ln1_g, ln1_b,
           w_router, b_router, w_exp_in, b_exp_in, w_exp_out, b_exp_out, ln2_g, ln2_b):
    bsz, s, d = x.shape
    assert bsz == 1 and d == D_MODEL and w_in.shape[0] == 1
    assert s % (DIL_CONFIGS[-1][1] * DBLK) == 0
    x2 = x[0]
    w_in0 = w_in[0]

    fg_lanes = np.array([FG_PAIR_STRIDE * (h // 2) + FG_PIECES * (h % 2) + j
                         for h in range(FOX_HEADS) for j in range(FG_PIECES)])
    fg_heads = np.repeat(np.arange(FOX_HEADS), FG_PIECES)
    w_fg = jnp.zeros((D_MODEL, LANES), F32).at[:, fg_lanes].set(w_in0[:, OFF_FOX_F + fg_heads])
    bf = jnp.zeros((1, LANES), F32).at[0, fg_lanes].set(b_fgate[0][fg_heads])
    w_all = jnp.concatenate([w_in0[:, OFF_FOX_QKV:OFF_FOX_QKV + 2 * FOX_W], w_in0[:, OFF_DIL_QKV:OFF_GATES],
                             w_fg], axis=1).astype(BF16)
    wvt = w_in0[:, OFF_FOX_QKV + 2 * FOX_W:OFF_FOX_F].T.astype(BF16)
    w_gates = w_in0[:, OFF_GATES:].astype(BF16)

    qk, ext, vt, o_mem, *dil_qkv = _proj(x2, w_all, wvt, bf, mem[0].astype(BF16), w_mem_kv[0].astype(BF16))

    o_fox = _fox(qk, ext, vt)

    ii = np.arange(DBLK, dtype=np.int32)[:, None]
    jj = np.arange(2 * DBLK, dtype=np.int32)[None, :]
    rel = np.clip(DBLK + ii - jj, 0, None)
    table_flat = t5_bias.reshape(-1)
    ods = []
    for g, (window, dil) in enumerate(DIL_CONFIGS):
        assert window // dil == DBLK
        bucket = _t5_bucket(jnp.asarray(rel * dil, dtype=jnp.int32))
        od = _dil(g, dil, table_flat, dil_qkv[g].reshape(s, DIL_QKV_W), bucket)
        ods.append(od.reshape(dil, s // dil, 2 * LANES))

    wr = jnp.zeros((D_MODEL, LANES), F32).at[:, :N_EXPERTS].set(w_router[0]).astype(BF16)
    br = jnp.zeros((1, LANES), F32).at[0, :N_EXPERTS].set(b_router[0])
    h1, h3, ri, rg, cnt = _merge(x2, o_fox, ods, o_mem, w_gates, w_br_fox[0].astype(BF16),
                                 w_br_dil[0].astype(BF16), w_br_mem[0].astype(BF16), w_out[0].astype(BF16),
                                 ln1_g, ln1_b, wr, br)

    tm = EXPERT_TM
    counts = cnt[0, :N_EXPERTS].astype(jnp.int32)
    padded = (counts + tm - 1) // tm * tm
    pad_ends = jnp.cumsum(padded)
    pad_starts = (pad_ends - padded).astype(jnp.int32)
    n_tiles = (s * TOP_K) // tm + N_EXPERTS
    n_used = (pad_ends[-1] // tm).astype(jnp.int32)
    tile_lo = jnp.arange(n_tiles, dtype=jnp.int32) * tm
    tile_e = jnp.minimum(jnp.sum((pad_ends[None, :] <= tile_lo[:, None]).astype(jnp.int32), axis=1),
                         N_EXPERTS - 1)
    tile_e = tile_e[jnp.minimum(jnp.arange(n_tiles), n_used - 1)]
    e_flat = ri[:, 0:TOP_K].reshape(-1)
    r_flat = ri[:, TOP_K:2 * TOP_K].reshape(-1)

    xb = _dispatch(e_flat, r_flat, pad_starts, h3, jnp.zeros((n_tiles * tm * SUBLANES, LANES), F32))
    yb = _experts(tile_e, n_used.reshape(1), xb, w_exp_in[0], b_exp_in[0][:, None, :], w_exp_out[0],
                  b_exp_out[0][:, None, :])
    out = _combine(e_flat, r_flat, pad_starts, h1, rg, ln2_g, ln2_b, yb)
    return out[None]
```

```python
import functools

import numpy as np
import jax
import jax.numpy as jnp
from jax import lax
from jax.experimental import pallas as pl
from jax.experimental.pallas import tpu as pltpu

D_MODEL = 1024
HEAD_DIM = 64
LANES = 128
SUBLANES = 8
assert D_MODEL == SUBLANES * LANES
FOX_HEADS = 6
FOX_PAIRS = FOX_HEADS // 2
DIL_CONFIGS = ((128, 1), (512, 4), (2048, 16))
DIL_GROUPS = len(DIL_CONFIGS)
DIL_HEADS = 2 * DIL_GROUPS
MEM_HEADS = 4
MEM_PAIRS = MEM_HEADS // 2
MEM_LEN = 256
FOX_W = FOX_HEADS * HEAD_DIM
DIL_W = DIL_HEADS * HEAD_DIM
MEM_W = MEM_HEADS * HEAD_DIM
DBLK = 128
DIL_QKV_W = 3 * LANES
T5_BUCKETS = 32
T5_MAX_EXACT = T5_BUCKETS // 2
T5_MAX_DISTANCE = 2048
N_EXPERTS = 32
TOP_K = 4
D_FF = D_MODEL
SWIGLU_LIMIT = 7.0
SWIGLU_ALPHA = 1.702
LN_EPS = 1e-5
DEEPNORM_ALPHA = 2.0 ** 0.25
QK_SCALE = HEAD_DIM ** -0.5

OFF_FOX_QKV = 0
OFF_FOX_F = OFF_FOX_QKV + 3 * FOX_W
OFF_DIL_QKV = OFF_FOX_F + FOX_HEADS
OFF_MEM_Q = OFF_DIL_QKV + 3 * DIL_W
OFF_GATES = OFF_MEM_Q + MEM_W

W_FOXQ = 0
W_FOXK = FOX_W
W_DIL = 2 * FOX_W
W_MEMQ = W_DIL + 3 * DIL_W
W_FGATE = W_MEMQ + MEM_W
W_ALL = W_FGATE + LANES
FG_PAIR_STRIDE = 8
FG_PIECES = 3
VT_ROWS = HEAD_DIM + 16

VMEM_LIMIT = 56 * 1024 * 1024

PROJ_TM = 512
FOX_TQ = 1024
FOX_TK = 512
FOX_TC = 256
DIL_CHUNK = 512
MERGE_TM = 512
DISPATCH_TM = 256
DISPATCH_SLOTS = 3
EXPERT_TM = 256
COMBINE_TM = 256
ROW_DMA_UNROLL = 4

BF16 = jnp.bfloat16
F32 = jnp.float32


def _dot(a, b):
    return jnp.dot(a, b, preferred_element_type=F32)


def _dot_nt(a, b):
    return lax.dot_general(a, b, (((1,), (1,)), ((), ())), preferred_element_type=F32)


def _lane_lt(shape, n):
    return lax.broadcasted_iota(jnp.int32, shape, len(shape) - 1) < n


def _split3(v):
    hi = v.astype(BF16)
    r1 = v - hi.astype(F32)
    mid = r1.astype(BF16)
    lo = (r1 - mid.astype(F32)).astype(BF16)
    return hi, mid, lo


def _store_token_tiles(ref, v):
    n = v.shape[0]
    for c in range(SUBLANES):
        ref[pl.ds(c, n, stride=SUBLANES), :] = v[:, c * LANES:(c + 1) * LANES]


def _load_token_tiles(ref, lo, n):
    return jnp.concatenate([ref[pl.ds(lo * SUBLANES + c, n, stride=SUBLANES), :] for c in range(SUBLANES)],
                           axis=1)


def _layer_norm(r, g, b):
    mu = jnp.mean(r, axis=-1, keepdims=True)
    d = r - mu
    var = jnp.mean(d * d, axis=-1, keepdims=True)
    return d * lax.rsqrt(var + LN_EPS) * g + b


def _proj_kernel(x_ref, w_ref, wvt_ref, bf_ref, mem_ref, wkv_ref, qk_ref, ext_ref, vt_ref, omem_ref,
                 d0_ref, d1_ref, d2_ref, kvm_ref, carry_ref, stage_ref):
    i = pl.program_id(0)
    tm = x_ref.shape[0]

    @pl.when(i == 0)
    def _():
        kvm_ref[...] = _dot(mem_ref[...], wkv_ref[...]).astype(BF16)
        carry_ref[...] = jnp.zeros_like(carry_ref)

    xb = x_ref[...].astype(BF16)
    proj = _dot(xb, w_ref[...])

    qk_ref[:, W_FOXQ:W_FOXK] = (proj[:, W_FOXQ:W_FOXK] * QK_SCALE).astype(BF16)
    qk_ref[:, W_FOXK:W_DIL] = proj[:, W_FOXK:W_DIL].astype(BF16)

    for g, (d_ref, (_, dil)) in enumerate(zip((d0_ref, d1_ref, d2_ref), DIL_CONFIGS)):
        cols = [proj[:, W_DIL + part * DIL_W + g * LANES:W_DIL + part * DIL_W + (g + 1) * LANES]
                for part in range(3)]
        cols[0] = cols[0] * QK_SCALE
        if dil == 1:
            for part in range(3):
                d_ref[0, :, part * LANES:(part + 1) * LANES] = cols[part].astype(BF16)
        else:
            for part in range(3):
                stage_ref[part] = cols[part]
            for r in range(dil):
                for part in range(3):
                    d_ref[r, :, part * LANES:(part + 1) * LANES] = (
                        stage_ref[part, pl.ds(r, tm // dil, stride=dil), :].astype(BF16))

    vt = _dot_nt(wvt_ref[...], xb)
    for h in range(FOX_HEADS):
        vt_ref[h, 0:HEAD_DIM, :] = vt[h * HEAD_DIM:(h + 1) * HEAD_DIM, :].astype(BF16)
        vt_ref[h, HEAD_DIM:VT_ROWS, :] = jnp.ones((VT_ROWS - HEAD_DIM, tm), BF16)

    z = proj[:, W_FGATE:W_ALL] + bf_ref[...]
    logf = jnp.minimum(z, 0.0) - jnp.log1p(jnp.exp(-jnp.abs(z)))
    row = lax.broadcasted_iota(jnp.int32, (tm, tm), 0)
    col = lax.broadcasted_iota(jnp.int32, (tm, tm), 1)
    lower = jnp.where(col <= row, 1.0, 0.0).astype(BF16)
    hi, mid, lo = _split3(logf)
    c = (_dot(lower, lo) + _dot(lower, mid)) + _dot(lower, hi) + carry_ref[0:1, :]
    carry_ref[...] = jnp.broadcast_to(c[tm - 1:tm, :], carry_ref.shape)
    nhi, nmid, nlo = _split3(-c)
    lane = lax.broadcasted_iota(jnp.int32, (tm, LANES), 1)
    slot = lane & (FG_PAIR_STRIDE - 1)
    piece = jnp.where(slot >= FG_PIECES, slot - FG_PIECES, slot)
    used = (slot < 2 * FG_PIECES) & (lane < FOX_PAIRS * FG_PAIR_STRIDE)
    ext = jnp.where(piece == 0, nhi, jnp.where(piece == 1, nmid, nlo))
    ext_ref[...] = jnp.where(used, ext, jnp.zeros_like(ext))

    qm = (proj[:, W_MEMQ:W_FGATE] * QK_SCALE).astype(BF16)
    for p in range(MEM_PAIRS):
        qp = qm[:, p * LANES:(p + 1) * LANES]
        kp = kvm_ref[:, p * LANES:(p + 1) * LANES]
        vp = kvm_ref[:, MEM_W + p * LANES:MEM_W + (p + 1) * LANES]
        first = _lane_lt(qp.shape, HEAD_DIM)
        outs = []
        for a in range(2):
            qa = jnp.where(first if a == 0 else jnp.logical_not(first), qp, jnp.zeros_like(qp))
            s = _dot_nt(qa, kp)
            m = jnp.max(s, axis=-1, keepdims=True)
            e = jnp.exp(s - m)
            den = jnp.sum(e, axis=-1, keepdims=True)
            outs.append(_dot(e.astype(BF16), vp) / den)
        omem_ref[:, p * LANES:(p + 1) * LANES] = jnp.where(first, outs[0], outs[1]).astype(BF16)


def _proj(x2, w_all, wvt, bf, mem_b, wkv_b):
    s = x2.shape[0]
    tm = min(PROJ_TM, s)
    full = lambda shape: pl.BlockSpec(shape, lambda i: (0,) * len(shape))
    rows = lambda w: pl.BlockSpec((tm, w), lambda i: (i, 0))
    dil_spec = lambda dil: pl.BlockSpec((dil, tm // dil, DIL_QKV_W), lambda i: (0, i, 0))
    dil_shape = lambda dil: jax.ShapeDtypeStruct((dil, s // dil, DIL_QKV_W), BF16)
    dils = [dil for _, dil in DIL_CONFIGS]
    return pl.pallas_call(
        _proj_kernel,
        grid=(s // tm,),
        in_specs=[rows(D_MODEL),
                  full(w_all.shape), full(wvt.shape), full(bf.shape), full(mem_b.shape), full(wkv_b.shape)],
        out_specs=[rows(W_DIL), rows(LANES),
                   pl.BlockSpec((FOX_HEADS, VT_ROWS, tm), lambda i: (0, 0, i)),
                   rows(MEM_W)] + [dil_spec(d) for d in dils],
        out_shape=[jax.ShapeDtypeStruct((s, W_DIL), BF16),
                   jax.ShapeDtypeStruct((s, LANES), BF16),
                   jax.ShapeDtypeStruct((FOX_HEADS, VT_ROWS, s), BF16),
                   jax.ShapeDtypeStruct((s, MEM_W), BF16)] + [dil_shape(d) for d in dils],
        scratch_shapes=[pltpu.VMEM((MEM_LEN, 2 * MEM_W), BF16), pltpu.VMEM((8, LANES), F32),
                        pltpu.VMEM((3, tm, LANES), F32)],
        compiler_params=pltpu.CompilerParams(dimension_semantics=("arbitrary",),
                                             vmem_limit_bytes=VMEM_LIMIT),
        name="proj",
    )(x2, w_all, wvt, bf, mem_b, wkv_b)


def _fox_kernel(q_ref, k_ref, e_ref, vt_ref, o_ref, m_ref, acc_ref, qx_ref, s_ref):
    p = pl.program_id(0)
    i = pl.program_id(1)
    t = q_ref.shape[0]
    tk = min(FOX_TK, t)
    q = q_ref[...]
    lane = lax.broadcasted_iota(jnp.int32, (t, LANES), 1)
    for a in range(2):
        head = (lane < HEAD_DIM) if a == 0 else (lane >= HEAD_DIM)
        lo = FG_PAIR_STRIDE * p + FG_PIECES * a
        ones = jnp.where((lane >= lo) & (lane < lo + FG_PIECES), 1.0, 0.0).astype(BF16)
        qx_ref[a, :, 0:LANES] = jnp.where(head, q, jnp.zeros_like(q))
        qx_ref[a, :, LANES:2 * LANES] = ones

    m_ref[...] = jnp.full(m_ref.shape, -jnp.inf, F32)
    acc_ref[...] = jnp.zeros(acc_ref.shape, F32)

    tc = min(FOX_TC, t)

    def phase(nxt, cur):
        if nxt is not None:
            nblk, nq0, nbuf = nxt
            nstart = pl.multiple_of(nblk * tk, tk)
            kx = jnp.concatenate([k_ref[pl.ds(nstart, tk), :], e_ref[pl.ds(nstart, tk), :]], axis=1)
        if cur is not None:
            cblk, cq0, cbuf, diagonal = cur
            cstart = pl.multiple_of(cblk * tk, tk)
            vtb = [vt_ref[a, :, pl.ds(cstart, tk)] for a in range(2)]
        for c in range(0, t, tc):
            cols = slice(c, c + tc)
            if nxt is not None and c >= nq0:
                for a in range(2):
                    s_ref[nbuf, a, :, cols] = _dot_nt(kx, qx_ref[a, cols, :])
            if cur is not None and c >= cq0:
                for a in range(2):
                    s = s_ref[cbuf, a, :, cols]
                    if diagonal:
                        kpos = lax.broadcasted_iota(jnp.int32, s.shape, 0) + cq0
                        qpos = lax.broadcasted_iota(jnp.int32, s.shape, 1) + c
                        s = jnp.where(kpos <= qpos, s, -jnp.inf)
                    m_prev = m_ref[a, :, cols]
                    m_new = jnp.maximum(m_prev, jnp.max(s, axis=0, keepdims=True))
                    alpha = jnp.exp(m_prev - m_new)
                    pt = jnp.exp(s - m_new).astype(BF16)
                    m_ref[a, :, cols] = m_new
                    acc_ref[a, :, cols] = alpha * acc_ref[a, :, cols] + _dot(vtb[a], pt)

    nd = t // tk
    nfull = i * nd
    phase((0, 0, 0), None)

    def pair(j, _):
        phase((2 * j + 1, 0, 1), (2 * j, 0, 0, False))
        phase((2 * j + 2, 0, 0), (2 * j + 1, 0, 1, False))
        return 0

    lax.fori_loop(0, nfull // 2, pair, 0)
    for d in range(nd):
        nxt = (nfull + d + 1, (d + 1) * tk, (d + 1) % 2) if d + 1 < nd else None
        phase(nxt, (nfull + d, d * tk, d % 2, True))
    o0 = acc_ref[0, 0:HEAD_DIM, :] / acc_ref[0, HEAD_DIM:HEAD_DIM + 1, :]
    o1 = acc_ref[1, 0:HEAD_DIM, :] / acc_ref[1, HEAD_DIM:HEAD_DIM + 1, :]
    o_ref[...] = jnp.concatenate([o0, o1], axis=0).T.astype(BF16)


def _fox(qk, ext, vt):
    s = qk.shape[0]
    t = min(FOX_TQ, s)
    tk = min(FOX_TK, t)
    assert s % t == 0 and t % (2 * tk) == 0
    kcol = W_FOXK // LANES
    return pl.pallas_call(
        _fox_kernel,
        grid=(FOX_PAIRS, s // t),
        in_specs=[pl.BlockSpec((t, LANES), lambda p, i: (i, p)),
                  pl.BlockSpec((s, LANES), lambda p, i: (0, kcol + p)),
                  pl.BlockSpec((s, LANES), lambda p, i: (0, 0)),
                  pl.BlockSpec((2, VT_ROWS, s), lambda p, i: (p, 0, 0))],
        out_specs=pl.BlockSpec((t, LANES), lambda p, i: (i, p)),
        out_shape=jax.ShapeDtypeStruct((s, FOX_W), BF16),
        scratch_shapes=[pltpu.VMEM((2, 1, t), F32), pltpu.VMEM((2, VT_ROWS, t), F32),
                        pltpu.VMEM((2, t, 2 * LANES), BF16), pltpu.VMEM((2, 2, tk, t), F32)],
        compiler_params=pltpu.CompilerParams(dimension_semantics=("arbitrary", "arbitrary"),
                                             vmem_limit_bytes=VMEM_LIMIT),
        name="fox",
    )(qk, qk, ext, vt)


def _dil_kernel(table_ref, cur_ref, prev_ref, bucket_ref, o_ref, bias_ref, *, group, blocks_per_seq):
    j = pl.program_id(0)
    nblk = cur_ref.shape[0] // DBLK
    ii = lax.broadcasted_iota(jnp.int32, (DBLK, 2 * DBLK), 0)
    jj = lax.broadcasted_iota(jnp.int32, (DBLK, 2 * DBLK), 1)

    @pl.when(j == 0)
    def _():
        bucket = bucket_ref[...]
        rel = DBLK + ii - jj
        in_window = (rel >= 0) & (rel <= DBLK)
        for a in range(2):
            b = jnp.zeros(bucket.shape, F32)
            for t in range(T5_BUCKETS):
                b = jnp.where(bucket == t, table_ref[t * DIL_HEADS + 2 * group + a], b)
            bias_ref[a] = jnp.where(in_window, b, -jnp.inf)

    first = _lane_lt((DBLK, LANES), HEAD_DIM)
    chunk_has_prev = ((j * nblk) % blocks_per_seq) != 0
    for b in range(nblk):
        rows = slice(b * DBLK, (b + 1) * DBLK)
        q = cur_ref[rows, 0:LANES]
        if b == 0:
            kprev, vprev = prev_ref[:, LANES:2 * LANES], prev_ref[:, 2 * LANES:3 * LANES]
        else:
            before = slice((b - 1) * DBLK, b * DBLK)
            kprev, vprev = cur_ref[before, LANES:2 * LANES], cur_ref[before, 2 * LANES:3 * LANES]
        kcat = jnp.concatenate([kprev, cur_ref[rows, LANES:2 * LANES]], axis=0)
        vcat = jnp.concatenate([vprev, cur_ref[rows, 2 * LANES:3 * LANES]], axis=0)
        outs, lses = [], []
        for a in range(2):
            qa = jnp.where(first if a == 0 else jnp.logical_not(first), q, jnp.zeros_like(q))
            s = _dot_nt(qa, kcat) + bias_ref[a]
            if b == 0:
                s = jnp.where(chunk_has_prev | (jj >= DBLK), s, -jnp.inf)
            m = jnp.max(s, axis=-1, keepdims=True)
            e = jnp.exp(s - m)
            den = jnp.sum(e, axis=-1, keepdims=True)
            outs.append(_dot(e.astype(BF16), vcat) / den)
            lses.append(jnp.broadcast_to(m + jnp.log(den), (DBLK, LANES)))
        o_ref[rows, 0:LANES] = jnp.where(first, outs[0], outs[1])
        o_ref[rows, LANES:2 * LANES] = jnp.where(first, lses[0], lses[1])


def _dil(group, dil, table_flat, qkv_sub, bucket):
    s = qkv_sub.shape[0]
    seq = s // dil
    chunk = min(DIL_CHUNK, seq)
    assert seq % chunk == 0 and chunk % DBLK == 0
    nblk = chunk // DBLK
    body = functools.partial(_dil_kernel, group=group, blocks_per_seq=seq // DBLK)
    return pl.pallas_call(
        body,
        grid=(s // chunk,),
        in_specs=[pl.BlockSpec(memory_space=pltpu.SMEM),
                  pl.BlockSpec((chunk, DIL_QKV_W), lambda j: (j, 0)),
                  pl.BlockSpec((DBLK, DIL_QKV_W), lambda j: (jnp.maximum(j * nblk - 1, 0), 0)),
                  pl.BlockSpec((DBLK, 2 * DBLK), lambda j: (0, 0))],
        out_specs=pl.BlockSpec((chunk, 2 * LANES), lambda j: (j, 0)),
        out_shape=jax.ShapeDtypeStruct((s, 2 * LANES), F32),
        scratch_shapes=[pltpu.VMEM((2, DBLK, 2 * DBLK), F32)],
        compiler_params=pltpu.CompilerParams(dimension_semantics=("arbitrary",),
                                             vmem_limit_bytes=VMEM_LIMIT),
        name=f"dil{group}",
    )(table_flat, qkv_sub, qkv_sub, bucket)


def _t5_bucket(dist):
    is_small = dist < T5_MAX_EXACT
    nf = jnp.maximum(dist, T5_MAX_EXACT).astype(F32)
    large = T5_MAX_EXACT + (jnp.log(nf / T5_MAX_EXACT) / np.log(T5_MAX_DISTANCE / T5_MAX_EXACT)
                            * (T5_BUCKETS - T5_MAX_EXACT)).astype(jnp.int32)
    large = jnp.minimum(large, T5_BUCKETS - 1)
    return jnp.where(is_small, dist, large)


def _merge_kernel(x_ref, ofox_ref, od0_ref, od1_ref, od2_ref, omem_ref, wg_ref, wbf_ref, wbd_ref, wbm_ref,
                  wout_ref, g_ref, b_ref, wr_ref, br_ref, h_ref, h3_ref, ri_ref, rg_ref, cnt_ref,
                  carry_ref, tok_ref):
    i = pl.program_id(0)
    tm = x_ref.shape[0]

    @pl.when(i == 0)
    def _():
        carry_ref[...] = jnp.zeros_like(carry_ref)

    x = x_ref[...]
    xb = x.astype(BF16)

    def gate(br):
        return jax.nn.sigmoid(_dot(xb, wg_ref[:, br * D_MODEL:(br + 1) * D_MODEL]))

    merged = gate(0) * _dot(ofox_ref[...], wbf_ref[...])

    for g, (od_ref, (_, dil)) in enumerate(zip((od0_ref, od1_ref, od2_ref), DIL_CONFIGS)):
        for r in range(dil):
            for half in range(2):
                tok_ref[g, half, pl.ds(r, tm // dil, stride=dil), :] = od_ref[r, :, half * LANES:(half + 1) * LANES]
    lse = [tok_ref[g, 1] for g in range(DIL_GROUPS)]
    mx = jnp.maximum(jnp.maximum(lse[0], lse[1]), lse[2])
    ex = [jnp.exp(l - mx) for l in lse]
    den = ex[0] + ex[1] + ex[2]
    bd = None
    for g in range(DIL_GROUPS):
        og = (tok_ref[g, 0] * (ex[g] / den)).astype(BF16)
        t = _dot(og, wbd_ref[g * LANES:(g + 1) * LANES, :])
        bd = t if bd is None else bd + t
    merged = merged + gate(1) * bd
    merged = merged + gate(2) * _dot(omem_ref[...], wbm_ref[...])

    y = _dot(merged.astype(BF16), wout_ref[...])
    h = _layer_norm(DEEPNORM_ALPHA * x + y, g_ref[...], b_ref[...])
    h_ref[...] = h
    _store_token_tiles(h3_ref, h)

    lane = lax.broadcasted_iota(jnp.int32, (tm, LANES), 1)
    logits = _dot(h.astype(BF16), wr_ref[...]) + br_ref[...]
    vals = jnp.where(lane < N_EXPERTS, logits, -jnp.inf)
    tops, hots = [], []
    for _ in range(TOP_K):
        mk = jnp.max(vals, axis=-1, keepdims=True)
        ik = jnp.min(jnp.where(vals == mk, lane, LANES), axis=-1, keepdims=True)
        hot = lane == ik
        vals = jnp.where(hot, -jnp.inf, vals)
        tops.append((mk, ik))
        hots.append(hot)
    es = [jnp.exp(mk - tops[0][0]) for mk, _ in tops]
    esum = es[0] + es[1] + es[2] + es[3]

    picked = jnp.where(hots[0] | hots[1] | hots[2] | hots[3], 1.0, 0.0)
    row = lax.broadcasted_iota(jnp.int32, (tm, tm), 0)
    col = lax.broadcasted_iota(jnp.int32, (tm, tm), 1)
    lower = jnp.where(col < row, 1.0, 0.0).astype(BF16)
    before = _dot(lower, picked.astype(BF16)) + carry_ref[0:1, :]
    ri = jnp.zeros((tm, LANES), jnp.int32)
    rg = jnp.zeros((tm, LANES), F32)
    for k in range(TOP_K):
        rank = jnp.sum(jnp.where(hots[k], before, 0.0), axis=-1, keepdims=True)
        ri = jnp.where(lane == k, tops[k][1], ri)
        ri = jnp.where(lane == TOP_K + k, rank.astype(jnp.int32), ri)
        rg = jnp.where(lane == k, es[k] / esum, rg)
    ri_ref[...] = ri
    rg_ref[...] = rg
    total = carry_ref[...] + jnp.sum(picked, axis=0, keepdims=True)
    carry_ref[...] = total
    cnt_ref[...] = total


def _merge(x2, ofox, ods, omem, wg, wbf, wbd, wbm, wout, g1, b1, wr, br):
    s = x2.shape[0]
    tm = min(MERGE_TM, s)
    full = lambda a: pl.BlockSpec(a.shape, lambda i: (0,) * a.ndim)
    rows = lambda w: pl.BlockSpec((tm, w), lambda i: (i, 0))
    od_specs = [pl.BlockSpec((dil, tm // dil, 2 * LANES), lambda i: (0, i, 0)) for _, dil in DIL_CONFIGS]
    return pl.pallas_call(
        _merge_kernel,
        grid=(s // tm,),
        in_specs=[rows(D_MODEL), rows(FOX_W)] + od_specs +
                 [rows(MEM_W), full(wg), full(wbf), full(wbd), full(wbm), full(wout),
                  full(g1), full(b1), full(wr), full(br)],
        out_specs=[rows(D_MODEL), pl.BlockSpec((tm * SUBLANES, LANES), lambda i: (i, 0)),
                   rows(LANES), rows(LANES), pl.BlockSpec((8, LANES), lambda i: (0, 0))],
        out_shape=[jax.ShapeDtypeStruct((s, D_MODEL), F32),
                   jax.ShapeDtypeStruct((s * SUBLANES, LANES), F32),
                   jax.ShapeDtypeStruct((s, LANES), jnp.int32),
                   jax.ShapeDtypeStruct((s, LANES), F32),
                   jax.ShapeDtypeStruct((8, LANES), F32)],
        scratch_shapes=[pltpu.VMEM((8, LANES), F32), pltpu.VMEM((DIL_GROUPS, 2, tm, LANES), F32)],
        compiler_params=pltpu.CompilerParams(dimension_semantics=("arbitrary",),
                                             vmem_limit_bytes=VMEM_LIMIT),
        name="merge",
    )(x2, ofox, *ods, omem, wg, wbf, wbd, wbm, wout, g1, b1, wr, br)


def _row_copy(src_ref, src_off, dst_ref, dst_off, sem):
    tile = lambda ref, off: ref.at[pl.ds(pl.multiple_of(off, SUBLANES), SUBLANES), :]
    return pltpu.make_async_copy(tile(src_ref, src_off), tile(dst_ref, dst_off), sem)


def _dispatch_kernel(dst_ref, lo_ref, hi_ref, nu_ref, h_ref, xb_ref, zero_ref, hbuf_ref, sem, zsem, lsem, *,
                     s_tiles):
    i = pl.program_id(0)
    n = pl.num_programs(0)
    rows = hbuf_ref.shape[0] // DISPATCH_SLOTS
    tm = rows // SUBLANES
    tile_rows = zero_ref.shape[0]
    n_tiles = xb_ref.shape[0] // tile_rows

    def tail_tile(b):
        return pltpu.make_async_copy(
            zero_ref, xb_ref.at[pl.ds(pl.multiple_of(b * tile_rows, tile_rows), tile_rows), :], zsem)

    def zero_fill(wait):
        def pad_rows(e, _):
            def one(r, _):
                cp = _row_copy(zero_ref, 0, xb_ref, r * SUBLANES, zsem)
                cp.wait() if wait else cp.start()
                return 0
            return lax.fori_loop(lo_ref[e], hi_ref[e], one, 0)
        lax.fori_loop(0, N_EXPERTS, pad_rows, 0)

        def tail(b, _):
            tail_tile(b).wait() if wait else tail_tile(b).start()
            return 0
        lax.fori_loop(nu_ref[0], n_tiles, tail, 0)

    def load(tile, slot):
        return pltpu.make_async_copy(h_ref.at[pl.ds(pl.multiple_of(tile * rows, rows), rows), :],
                                     hbuf_ref.at[pl.ds(pl.multiple_of(slot * rows, rows), rows), :],
                                     lsem.at[slot])

    def wait_scatter(slot):
        def wait(t, _):
            for k in range(TOP_K):
                _row_copy(hbuf_ref, 0, xb_ref, 0, sem.at[slot]).wait()
            return 0
        lax.fori_loop(0, tm, wait, 0, unroll=ROW_DMA_UNROLL)

    slot = i % DISPATCH_SLOTS
    nxt = (i + 1) % DISPATCH_SLOTS

    @pl.when(i == 0)
    def _():
        load(0, 0).start()
        zero_ref[...] = jnp.zeros(zero_ref.shape, F32)
        zero_fill(False)
        zero_fill(True)

    load(i, slot).wait()

    @pl.when(i >= DISPATCH_SLOTS - 1)
    def _():
        wait_scatter(nxt)

    @pl.when(i + 1 < n)
    def _():
        load(i + 1, nxt).start()

    def start(t, _):
        for k in range(TOP_K):
            _row_copy(hbuf_ref, slot * rows + t * SUBLANES, xb_ref, dst_ref[t * TOP_K + k],
                      sem.at[slot]).start(priority=k % 2)
        return 0

    lax.fori_loop(0, tm, start, 0, unroll=ROW_DMA_UNROLL)

    @pl.when(i == n - 1)
    def _():
        for back in range(min(DISPATCH_SLOTS - 1, s_tiles)):
            wait_scatter((i - back) % DISPATCH_SLOTS)


def _dispatch(dst8, pad_lo, pad_hi, n_used, h3, n_tiles, tile_tokens):
    s = h3.shape[0] // SUBLANES
    tm = min(DISPATCH_TM, s)
    smem = pl.BlockSpec(memory_space=pltpu.SMEM)
    body = functools.partial(_dispatch_kernel, s_tiles=s // tm)
    return pl.pallas_call(
        body,
        grid=(s // tm,),
        in_specs=[pl.BlockSpec((tm * TOP_K,), lambda i: (i,), memory_space=pltpu.SMEM), smem, smem, smem,
                  pl.BlockSpec(memory_space=pl.ANY)],
        out_specs=pl.BlockSpec(memory_space=pl.ANY),
        out_shape=jax.ShapeDtypeStruct((n_tiles * tile_tokens * SUBLANES, LANES), F32),
        scratch_shapes=[pltpu.VMEM((tile_tokens * SUBLANES, LANES), F32),
                        pltpu.VMEM((DISPATCH_SLOTS * tm * SUBLANES, LANES), F32),
                        pltpu.SemaphoreType.DMA((DISPATCH_SLOTS,)), pltpu.SemaphoreType.DMA(()),
                        pltpu.SemaphoreType.DMA((DISPATCH_SLOTS,))],
        compiler_params=pltpu.CompilerParams(dimension_semantics=("arbitrary",),
                                             vmem_limit_bytes=VMEM_LIMIT),
        name="dispatch",
    )(dst8, pad_lo, pad_hi, n_used, h3)


def _expert_kernel(te_ref, nu_ref, x_ref, w1_ref, b1_ref, w2_ref, b2_ref, y_ref, w1b_ref, w2b_ref):
    b = pl.program_id(0)

    @pl.when(b >= nu_ref[0])
    def _():
        y_ref[...] = jnp.zeros(y_ref.shape, y_ref.dtype)

    @pl.when(b < nu_ref[0])
    def _():
        prev = te_ref[jnp.maximum(b - 1, 0)]

        @pl.when((b == 0) | (te_ref[b] != prev))
        def _():
            chunk = 128
            def cast(r, _):
                rows = pl.ds(pl.multiple_of(r * chunk, chunk), chunk)
                w1b_ref[rows, :] = w1_ref[rows, :].astype(BF16)
                w2b_ref[rows, :] = w2_ref[rows, :].astype(BF16)
                return 0
            lax.fori_loop(0, D_MODEL // chunk, cast, 0)

        xe = _load_token_tiles(x_ref, 0, x_ref.shape[0] // SUBLANES).astype(BF16)
        hcat = _dot(xe, w1b_ref[...]) + b1_ref[...]
        h_glu = jnp.minimum(hcat[:, :D_FF], SWIGLU_LIMIT)
        h_lin = jnp.clip(hcat[:, D_FF:], -SWIGLU_LIMIT, SWIGLU_LIMIT)
        act = h_glu * jax.nn.sigmoid(SWIGLU_ALPHA * h_glu) * (h_lin + 1.0)
        _store_token_tiles(y_ref, _dot(act.astype(BF16), w2b_ref[...]) + b2_ref[...])


def _experts(tile_e, n_used, xb, w1, b1, w2, b2):
    p = xb.shape[0] // SUBLANES
    tm = EXPERT_TM
    row_map = lambda b, te, nu: (jnp.minimum(b, nu[0] - 1), 0)
    exp_map = lambda b, te, nu: (te[b], 0, 0)
    return pl.pallas_call(
        _expert_kernel,
        grid_spec=pltpu.PrefetchScalarGridSpec(
            num_scalar_prefetch=2,
            grid=(p // tm,),
            in_specs=[pl.BlockSpec((tm * SUBLANES, LANES), row_map),
                      pl.BlockSpec((None, D_MODEL, 2 * D_FF), exp_map),
                      pl.BlockSpec((None, 1, 2 * D_FF), exp_map),
                      pl.BlockSpec((None, D_FF, D_MODEL), exp_map),
                      pl.BlockSpec((None, 1, D_MODEL), exp_map)],
            out_specs=pl.BlockSpec((tm * SUBLANES, LANES), lambda b, te, nu: (b, 0)),
            scratch_shapes=[pltpu.VMEM((D_MODEL, 2 * D_FF), BF16), pltpu.VMEM((D_FF, D_MODEL), BF16)]),
        out_shape=jax.ShapeDtypeStruct((p * SUBLANES, LANES), F32),
        compiler_params=pltpu.CompilerParams(dimension_semantics=("arbitrary",),
                                             vmem_limit_bytes=VMEM_LIMIT),
        name="experts",
    )(tile_e, n_used, xb, w1, b1, w2, b2)


def _combine_kernel(cur_ref, nxt_ref, h_ref, rg_ref, g_ref, b_ref, yb_ref, o_ref, buf_ref, sem):
    i = pl.program_id(0)
    tm = h_ref.shape[0]
    slot_rows = TOP_K * tm * SUBLANES
    slot = i % 2

    def gather(src_ref, into):
        def start(t, _):
            for k in range(TOP_K):
                _row_copy(yb_ref, src_ref[t * TOP_K + k], buf_ref, into * slot_rows + (k * tm + t) * SUBLANES,
                          sem.at[into]).start(priority=k % 2)
            return 0
        lax.fori_loop(0, tm, start, 0, unroll=ROW_DMA_UNROLL)

    @pl.when(i == 0)
    def _():
        gather(cur_ref, 0)

    def wait(t, _):
        for k in range(TOP_K):
            _row_copy(yb_ref, 0, buf_ref, 0, sem.at[slot]).wait()
        return 0

    lax.fori_loop(0, tm, wait, 0, unroll=ROW_DMA_UNROLL)

    @pl.when(i + 1 < pl.num_programs(0))
    def _():
        gather(nxt_ref, 1 - slot)

    h = h_ref[...]
    rg = rg_ref[...]
    moe = None
    for k in range(TOP_K):
        t = rg[:, k:k + 1] * _load_token_tiles(buf_ref, slot * (TOP_K * tm) + k * tm, tm)
        moe = t if moe is None else moe + t
    o_ref[...] = _layer_norm(DEEPNORM_ALPHA * h + moe, g_ref[...], b_ref[...])


def _combine(dst8, h, rg, g2, b2, yb):
    s = h.shape[0]
    tm = min(COMBINE_TM, s)
    last = s // tm - 1
    full = lambda a: pl.BlockSpec(a.shape, lambda i: (0,) * a.ndim)
    return pl.pallas_call(
        _combine_kernel,
        grid=(s // tm,),
        in_specs=[pl.BlockSpec((tm * TOP_K,), lambda i: (i,), memory_space=pltpu.SMEM),
                  pl.BlockSpec((tm * TOP_K,), lambda i: (jnp.minimum(i + 1, last),), memory_space=pltpu.SMEM),
                  pl.BlockSpec((tm, D_MODEL), lambda i: (i, 0)),
                  pl.BlockSpec((tm, LANES), lambda i: (i, 0)),
                  full(g2), full(b2),
                  pl.BlockSpec(memory_space=pl.ANY)],
        out_specs=pl.BlockSpec((tm, D_MODEL), lambda i: (i, 0)),
        out_shape=jax.ShapeDtypeStruct((s, D_MODEL), F32),
        scratch_shapes=[pltpu.VMEM((2 * TOP_K * tm * SUBLANES, LANES), F32), pltpu.SemaphoreType.DMA((2,))],
        compiler_params=pltpu.CompilerParams(dimension_semantics=("arbitrary",),
                                             vmem_limit_bytes=VMEM_LIMIT),
        name="combine",
    )(dst8, dst8, h, rg, g2, b2, yb)


def kernel(x, mem, w_in, b_fgate, t5_bias, w_mem_kv, w_br_fox, w_br_dil, w_br_mem, w_out, ln1_g, ln1_b,
           w_router, b_router, w_exp_in, b_exp_in, w_exp_out, b_exp_out, ln2_g, ln2_b):
    bsz, s, d = x.shape
    assert bsz == 1 and d == D_MODEL and w_in.shape[0] == 1
    assert s % (DIL_CONFIGS[-1][1] * DBLK) == 0
    x2 = x[0]
    w_in0 = w_in[0]

    fg_lanes = np.array([FG_PAIR_STRIDE * (h // 2) + FG_PIECES * (h % 2) + j
                         for h in range(FOX_HEADS) for j in range(FG_PIECES)])
    fg_heads = np.repeat(np.arange(FOX_HEADS), FG_PIECES)
    w_fg = jnp.zeros((D_MODEL, LANES), F32).at[:, fg_lanes].set(w_in0[:, OFF_FOX_F + fg_heads])
    bf = jnp.zeros((1, LANES), F32).at[0, fg_lanes].set(b_fgate[0][fg_heads])
    w_all = jnp.concatenate([w_in0[:, OFF_FOX_QKV:OFF_FOX_QKV + 2 * FOX_W], w_in0[:, OFF_DIL_QKV:OFF_GATES],
                             w_fg], axis=1).astype(BF16)
    wvt = w_in0[:, OFF_FOX_QKV + 2 * FOX_W:OFF_FOX_F].T.astype(BF16)
    w_gates = w_in0[:, OFF_GATES:].astype(BF16)

    qk, ext, vt, o_mem, *dil_qkv = _proj(x2, w_all, wvt, bf, mem[0].astype(BF16), w_mem_kv[0].astype(BF16))

    o_fox = _fox(qk, ext, vt)

    ii = np.arange(DBLK, dtype=np.int32)[:, None]
    jj = np.arange(2 * DBLK, dtype=np.int32)[None, :]
    rel = np.clip(DBLK + ii - jj, 0, None)
    table_flat = t5_bias.reshape(-1)
    ods = []
    for g, (window, dil) in enumerate(DIL_CONFIGS):
        assert window // dil == DBLK
        bucket = _t5_bucket(jnp.asarray(rel * dil, dtype=jnp.int32))
        od = _dil(g, dil, table_flat, dil_qkv[g].reshape(s, DIL_QKV_W), bucket)
        ods.append(od.reshape(dil, s // dil, 2 * LANES))

    wr = jnp.zeros((D_MODEL, LANES), F32).at[:, :N_EXPERTS].set(w_router[0]).astype(BF16)
    br = jnp.zeros((1, LANES), F32).at[0, :N_EXPERTS].set(b_router[0])
    h1, h3, ri, rg, cnt = _merge(x2, o_fox, ods, o_mem, w_gates, w_br_fox[0].astype(BF16),
                                 w_br_dil[0].astype(BF16), w_br_mem[0].astype(BF16), w_out[0].astype(BF16),
                                 ln1_g, ln1_b, wr, br)

    tm = EXPERT_TM
    counts = cnt[0, :N_EXPERTS].astype(jnp.int32)
    padded = (counts + tm - 1) // tm * tm
    pad_ends = jnp.cumsum(padded)
    pad_starts = (pad_ends - padded).astype(jnp.int32)
    n_tiles = (s * TOP_K) // tm + N_EXPERTS
    n_used = (pad_ends[-1] // tm).astype(jnp.int32)
    tile_lo = jnp.arange(n_tiles, dtype=jnp.int32) * tm
    tile_e = jnp.minimum(jnp.sum((pad_ends[None, :] <= tile_lo[:, None]).astype(jnp.int32), axis=1),
                         N_EXPERTS - 1)
    tile_e = tile_e[jnp.minimum(jnp.arange(n_tiles), n_used - 1)]
    e_tk = ri[:, 0:TOP_K]
    start_tk = jnp.sum(jnp.where(e_tk[:, :, None] == jnp.arange(N_EXPERTS, dtype=jnp.int32),
                                 pad_starts, 0), axis=-1)
    dst8 = ((start_tk + ri[:, TOP_K:2 * TOP_K]) * SUBLANES).reshape(-1)

    xb = _dispatch(dst8, pad_starts + counts, pad_ends.astype(jnp.int32), n_used.reshape(1), h3, n_tiles, tm)
    yb = _experts(tile_e, n_used.reshape(1), xb, w_exp_in[0], b_exp_in[0][:, None, :], w_exp_out[0],
                  b_exp_out[0][:, None, :])
    out = _combine(dst8, h1, rg, ln2_g, ln2_b, yb)
    return out[None]
```

```python
import functools

import numpy as np
import jax
import jax.numpy as jnp
from jax import lax
from jax.experimental import pallas as pl
from jax.experimental.pallas import tpu as pltpu

D_MODEL = 1024
HEAD_DIM = 64
LANES = 128
SUBLANES = 8
assert D_MODEL == SUBLANES * LANES
FOX_HEADS = 6
FOX_PAIRS = FOX_HEADS // 2
DIL_CONFIGS = ((128, 1), (512, 4), (2048, 16))
DIL_GROUPS = len(DIL_CONFIGS)
DIL_HEADS = 2 * DIL_GROUPS
MEM_HEADS = 4
MEM_PAIRS = MEM_HEADS // 2
MEM_LEN = 256
FOX_W = FOX_HEADS * HEAD_DIM
DIL_W = DIL_HEADS * HEAD_DIM
MEM_W = MEM_HEADS * HEAD_DIM
DBLK = 128
DIL_QKV_W = 3 * LANES
T5_BUCKETS = 32
T5_MAX_EXACT = T5_BUCKETS // 2
T5_MAX_DISTANCE = 2048
N_EXPERTS = 32
TOP_K = 4
D_FF = D_MODEL
SWIGLU_LIMIT = 7.0
SWIGLU_ALPHA = 1.702
LN_EPS = 1e-5
DEEPNORM_ALPHA = 2.0 ** 0.25
QK_SCALE = HEAD_DIM ** -0.5
F32_EXP_ZERO_BELOW = 110.0

OFF_FOX_QKV = 0
OFF_FOX_F = OFF_FOX_QKV + 3 * FOX_W
OFF_DIL_QKV = OFF_FOX_F + FOX_HEADS
OFF_MEM_Q = OFF_DIL_QKV + 3 * DIL_W
OFF_GATES = OFF_MEM_Q + MEM_W

W_FOXQ = 0
W_FOXK = FOX_W
W_DIL = 2 * FOX_W
W_MEMQ = W_DIL + 3 * DIL_W
W_FGATE = W_MEMQ + MEM_W
W_ALL = W_FGATE + LANES
FG_PAIR_STRIDE = 8
FG_PIECES = 3
VT_ROWS = HEAD_DIM + 16

VMEM_LIMIT = 56 * 1024 * 1024

PROJ_TM = 512
FOX_TQ = 1024
FOX_TK = 512
FOX_TC = 1024
DIL_CHUNK = 512
MERGE_TM = 512
DISPATCH_TM = 256
DISPATCH_SLOTS = 3
EXPERT_TM = 256
COMBINE_TM = 256
ROW_DMA_UNROLL = 4

BF16 = jnp.bfloat16
F32 = jnp.float32


def _dot(a, b):
    return jnp.dot(a, b, preferred_element_type=F32)


def _dot_nt(a, b):
    return lax.dot_general(a, b, (((1,), (1,)), ((), ())), preferred_element_type=F32)


def _lane_lt(shape, n):
    return lax.broadcasted_iota(jnp.int32, shape, len(shape) - 1) < n


def _split3(v):
    hi = v.astype(BF16)
    r1 = v - hi.astype(F32)
    mid = r1.astype(BF16)
    lo = (r1 - mid.astype(F32)).astype(BF16)
    return hi, mid, lo


def _store_token_tiles(ref, v):
    n = v.shape[0]
    for c in range(SUBLANES):
        ref[pl.ds(c, n, stride=SUBLANES), :] = v[:, c * LANES:(c + 1) * LANES]


def _load_token_tiles(ref, lo, n):
    return jnp.concatenate([ref[pl.ds(lo * SUBLANES + c, n, stride=SUBLANES), :] for c in range(SUBLANES)],
                           axis=1)


def _layer_norm(r, g, b):
    mu = jnp.mean(r, axis=-1, keepdims=True)
    d = r - mu
    var = jnp.mean(d * d, axis=-1, keepdims=True)
    return d * lax.rsqrt(var + LN_EPS) * g + b


def _proj_kernel(x_ref, w_ref, wvt_ref, bf_ref, mem_ref, wkv_ref, qk_ref, ext_ref, vt_ref, omem_ref,
                 d0_ref, d1_ref, d2_ref, stat_ref, kvm_ref, carry_ref, stage_ref):
    i = pl.program_id(0)
    tm = x_ref.shape[0]

    @pl.when(i == 0)
    def _():
        kvm_ref[...] = _dot(mem_ref[...], wkv_ref[...]).astype(BF16)
        carry_ref[...] = jnp.zeros_like(carry_ref)

    xb = x_ref[...].astype(BF16)
    proj = _dot(xb, w_ref[...])

    qb = (proj[:, W_FOXQ:W_FOXK] * QK_SCALE).astype(BF16)
    kb = proj[:, W_FOXK:W_DIL].astype(BF16)
    qk_ref[:, W_FOXQ:W_FOXK] = qb
    qk_ref[:, W_FOXK:W_DIL] = kb

    stat_ref[...] = jnp.zeros(stat_ref.shape, F32)
    first_head = _lane_lt((tm, LANES), HEAD_DIM)
    for which, vb in enumerate((qb, kb)):
        for p in range(FOX_PAIRS):
            sq = jnp.square(vb[:, p * LANES:(p + 1) * LANES].astype(F32))
            for a in range(2):
                mask = first_head if a == 0 else jnp.logical_not(first_head)
                norm2 = jnp.sum(jnp.where(mask, sq, 0.0), axis=1, keepdims=True)
                top = jnp.max(norm2, axis=0, keepdims=True)
                stat_ref[0, which, 2 * p + a:2 * p + a + 1, :] = jnp.broadcast_to(top, (1, LANES))

    for g, (d_ref, (_, dil)) in enumerate(zip((d0_ref, d1_ref, d2_ref), DIL_CONFIGS)):
        cols = [proj[:, W_DIL + part * DIL_W + g * LANES:W_DIL + part * DIL_W + (g + 1) * LANES]
                for part in range(3)]
        cols[0] = cols[0] * QK_SCALE
        if dil == 1:
            for part in range(3):
                d_ref[0, :, part * LANES:(part + 1) * LANES] = cols[part].astype(BF16)
        else:
            for part in range(3):
                stage_ref[part] = cols[part]
            for r in range(dil):
                for part in range(3):
                    d_ref[r, :, part * LANES:(part + 1) * LANES] = (
                        stage_ref[part, pl.ds(r, tm // dil, stride=dil), :].astype(BF16))

    vt = _dot_nt(wvt_ref[...], xb)
    for h in range(FOX_HEADS):
        vt_ref[h, 0:HEAD_DIM, :] = vt[h * HEAD_DIM:(h + 1) * HEAD_DIM, :].astype(BF16)
        vt_ref[h, HEAD_DIM:VT_ROWS, :] = jnp.ones((VT_ROWS - HEAD_DIM, tm), BF16)

    z = proj[:, W_FGATE:W_ALL] + bf_ref[...]
    logf = jnp.minimum(z, 0.0) - jnp.log1p(jnp.exp(-jnp.abs(z)))
    row = lax.broadcasted_iota(jnp.int32, (tm, tm), 0)
    col = lax.broadcasted_iota(jnp.int32, (tm, tm), 1)
    lower = jnp.where(col <= row, 1.0, 0.0).astype(BF16)
    hi, mid, lo = _split3(logf)
    c = (_dot(lower, lo) + _dot(lower, mid)) + _dot(lower, hi) + carry_ref[0:1, :]
    carry_ref[...] = jnp.broadcast_to(c[tm - 1:tm, :], carry_ref.shape)
    stat_ref[0, 2, 0:1, :] = jnp.max(-c, axis=0, keepdims=True)
    nhi, nmid, nlo = _split3(-c)
    lane = lax.broadcasted_iota(jnp.int32, (tm, LANES), 1)
    slot = lane & (FG_PAIR_STRIDE - 1)
    piece = jnp.where(slot >= FG_PIECES, slot - FG_PIECES, slot)
    used = (slot < 2 * FG_PIECES) & (lane < FOX_PAIRS * FG_PAIR_STRIDE)
    ext = jnp.where(piece == 0, nhi, jnp.where(piece == 1, nmid, nlo))
    ext_ref[...] = jnp.where(used, ext, jnp.zeros_like(ext))

    qm = (proj[:, W_MEMQ:W_FGATE] * QK_SCALE).astype(BF16)
    for p in range(MEM_PAIRS):
        qp = qm[:, p * LANES:(p + 1) * LANES]
        kp = kvm_ref[:, p * LANES:(p + 1) * LANES]
        vp = kvm_ref[:, MEM_W + p * LANES:MEM_W + (p + 1) * LANES]
        first = _lane_lt(qp.shape, HEAD_DIM)
        outs = []
        for a in range(2):
            qa = jnp.where(first if a == 0 else jnp.logical_not(first), qp, jnp.zeros_like(qp))
            s = _dot_nt(qa, kp)
            m = jnp.max(s, axis=-1, keepdims=True)
            e = jnp.exp(s - m)
            den = jnp.sum(e, axis=-1, keepdims=True)
            outs.append(_dot(e.astype(BF16), vp) / den)
        omem_ref[:, p * LANES:(p + 1) * LANES] = jnp.where(first, outs[0], outs[1]).astype(BF16)


def _proj(x2, w_all, wvt, bf, mem_b, wkv_b):
    s = x2.shape[0]
    tm = min(PROJ_TM, s)
    full = lambda shape: pl.BlockSpec(shape, lambda i: (0,) * len(shape))
    rows = lambda w: pl.BlockSpec((tm, w), lambda i: (i, 0))
    dil_spec = lambda dil: pl.BlockSpec((dil, tm // dil, DIL_QKV_W), lambda i: (0, i, 0))
    dil_shape = lambda dil: jax.ShapeDtypeStruct((dil, s // dil, DIL_QKV_W), BF16)
    dils = [dil for _, dil in DIL_CONFIGS]
    return pl.pallas_call(
        _proj_kernel,
        grid=(s // tm,),
        in_specs=[rows(D_MODEL),
                  full(w_all.shape), full(wvt.shape), full(bf.shape), full(mem_b.shape), full(wkv_b.shape)],
        out_specs=[rows(W_DIL), rows(LANES),
                   pl.BlockSpec((FOX_HEADS, VT_ROWS, tm), lambda i: (0, 0, i)),
                   rows(MEM_W)] + [dil_spec(d) for d in dils] +
                  [pl.BlockSpec((1, 3, SUBLANES, LANES), lambda i: (i, 0, 0, 0))],
        out_shape=[jax.ShapeDtypeStruct((s, W_DIL), BF16),
                   jax.ShapeDtypeStruct((s, LANES), BF16),
                   jax.ShapeDtypeStruct((FOX_HEADS, VT_ROWS, s), BF16),
                   jax.ShapeDtypeStruct((s, MEM_W), BF16)] + [dil_shape(d) for d in dils] +
                  [jax.ShapeDtypeStruct((s // tm, 3, SUBLANES, LANES), F32)],
        scratch_shapes=[pltpu.VMEM((MEM_LEN, 2 * MEM_W), BF16), pltpu.VMEM((8, LANES), F32),
                        pltpu.VMEM((3, tm, LANES), F32)],
        compiler_params=pltpu.CompilerParams(dimension_semantics=("arbitrary",),
                                             vmem_limit_bytes=VMEM_LIMIT),
        name="proj",
    )(x2, w_all, wvt, bf, mem_b, wkv_b)


def _fox_kernel(km_ref, nc_ref, qn_ref, q_ref, k_ref, e_ref, vt_ref, o_ref, m_ref, acc_ref, qx_ref, s_ref):
    p = pl.program_id(0)
    i = pl.program_id(1)
    t = q_ref.shape[0]
    tk = min(FOX_TK, t)
    q = q_ref[...]
    lane = lax.broadcasted_iota(jnp.int32, (t, LANES), 1)
    for a in range(2):
        head = (lane < HEAD_DIM) if a == 0 else (lane >= HEAD_DIM)
        lo = FG_PAIR_STRIDE * p + FG_PIECES * a
        ones = jnp.where((lane >= lo) & (lane < lo + FG_PIECES), 1.0, 0.0).astype(BF16)
        qx_ref[a, :, 0:LANES] = jnp.where(head, q, jnp.zeros_like(q))
        qx_ref[a, :, LANES:2 * LANES] = ones

    m_ref[...] = jnp.full(m_ref.shape, -jnp.inf, F32)
    acc_ref[...] = jnp.zeros(acc_ref.shape, F32)

    tc = min(FOX_TC, t)

    def phase(nxt, cur):
        if nxt is not None:
            nblk, nq0, nbuf = nxt
            nstart = pl.multiple_of(nblk * tk, tk)
            kx = jnp.concatenate([k_ref[pl.ds(nstart, tk), :], e_ref[pl.ds(nstart, tk), :]], axis=1)
        if cur is not None:
            cblk, cq0, cbuf, diagonal = cur
            cstart = pl.multiple_of(cblk * tk, tk)
            vtb = [vt_ref[a, :, pl.ds(cstart, tk)] for a in range(2)]
        for c in range(0, t, tc):
            if nxt is not None and c + tc > nq0:
                cols = slice(max(c, nq0), c + tc)
                for a in range(2):
                    s_ref[nbuf, a, :, cols] = _dot_nt(kx, qx_ref[a, cols, :])
            if cur is not None and c + tc > cq0:
                cols = slice(max(c, cq0), c + tc)
                for a in range(2):
                    s = s_ref[cbuf, a, :, cols]
                    if diagonal:
                        kpos = lax.broadcasted_iota(jnp.int32, s.shape, 0) + cq0
                        qpos = lax.broadcasted_iota(jnp.int32, s.shape, 1) + cols.start
                        s = jnp.where(kpos <= qpos, s, -jnp.inf)
                    m_prev = m_ref[a, :, cols]
                    m_new = jnp.maximum(m_prev, jnp.max(s, axis=0, keepdims=True))
                    alpha = jnp.exp(m_prev - m_new)
                    pt = jnp.exp(s - m_new).astype(BF16)
                    m_ref[a, :, cols] = m_new
                    acc_ref[a, :, cols] = alpha * acc_ref[a, :, cols] + _dot(vtb[a], pt)

    nd = t // tk
    nfull = i * nd
    phase((nfull, 0, 0), None)
    for d in range(nd):
        nxt = (nfull + d + 1, (d + 1) * tk, (d + 1) % 2) if d + 1 < nd else None
        phase(nxt, (nfull + d, d * tk, d % 2, True))

    nblocks = k_ref.shape[0] // tk
    nq = pl.num_programs(1)
    need = jnp.int32(0)
    for a in range(2):
        h = 2 * p + a
        thr = jnp.min(m_ref[a]) - F32_EXP_ZERO_BELOW
        qn = qn_ref[h * nq + i]
        count = lax.fori_loop(
            0, nfull,
            lambda j, n: n + (qn * km_ref[h * nblocks + j] + nc_ref[h * nblocks + j] >= thr).astype(jnp.int32),
            jnp.int32(0))
        need = jnp.maximum(need, count)
    need = jnp.minimum(jnp.bitwise_and(need + 1, -2), nfull)
    first = nfull - need

    @pl.when(need > 0)
    def _():
        phase((first, 0, 0), None)

        def pair(j, _):
            b0 = first + 2 * j
            phase((b0 + 1, 0, 1), (b0, 0, 0, False))
            phase((b0 + 2, 0, 0), (b0 + 1, 0, 1, False))
            return 0

        lax.fori_loop(0, need // 2 - 1, pair, 0)
        phase((nfull - 1, 0, 1), (nfull - 2, 0, 0, False))
        phase(None, (nfull - 1, 0, 1, False))
    o0 = acc_ref[0, 0:HEAD_DIM, :] / acc_ref[0, HEAD_DIM:HEAD_DIM + 1, :]
    o1 = acc_ref[1, 0:HEAD_DIM, :] / acc_ref[1, HEAD_DIM:HEAD_DIM + 1, :]
    o_ref[...] = jnp.concatenate([o0, o1], axis=0).T.astype(BF16)


def _fox(km, nc, qn, qk, ext, vt):
    s = qk.shape[0]
    t = min(FOX_TQ, s)
    tk = min(FOX_TK, t)
    assert s % t == 0 and t % (2 * tk) == 0
    kcol = W_FOXK // LANES
    return pl.pallas_call(
        _fox_kernel,
        grid=(FOX_PAIRS, s // t),
        in_specs=[pl.BlockSpec(memory_space=pltpu.SMEM), pl.BlockSpec(memory_space=pltpu.SMEM),
                  pl.BlockSpec(memory_space=pltpu.SMEM),
                  pl.BlockSpec((t, LANES), lambda p, i: (i, p)),
                  pl.BlockSpec((s, LANES), lambda p, i: (0, kcol + p)),
                  pl.BlockSpec((s, LANES), lambda p, i: (0, 0)),
                  pl.BlockSpec((2, VT_ROWS, s), lambda p, i: (p, 0, 0))],
        out_specs=pl.BlockSpec((t, LANES), lambda p, i: (i, p)),
        out_shape=jax.ShapeDtypeStruct((s, FOX_W), BF16),
        scratch_shapes=[pltpu.VMEM((2, 1, t), F32), pltpu.VMEM((2, VT_ROWS, t), F32),
                        pltpu.VMEM((2, t, 2 * LANES), BF16), pltpu.VMEM((2, 2, tk, t), F32)],
        compiler_params=pltpu.CompilerParams(dimension_semantics=("arbitrary", "arbitrary"),
                                             vmem_limit_bytes=VMEM_LIMIT),
        name="fox",
    )(km, nc, qn, qk, qk, ext, vt)


def _dil_kernel(table_ref, cur_ref, prev_ref, bucket_ref, o_ref, bias_ref, *, group, blocks_per_seq):
    j = pl.program_id(0)
    nblk = cur_ref.shape[0] // DBLK
    ii = lax.broadcasted_iota(jnp.int32, (DBLK, 2 * DBLK), 0)
    jj = lax.broadcasted_iota(jnp.int32, (DBLK, 2 * DBLK), 1)

    @pl.when(j == 0)
    def _():
        bucket = bucket_ref[...]
        rel = DBLK + ii - jj
        in_window = (rel >= 0) & (rel <= DBLK)
        for a in range(2):
            b = jnp.zeros(bucket.shape, F32)
            for t in range(T5_BUCKETS):
                b = jnp.where(bucket == t, table_ref[t * DIL_HEADS + 2 * group + a], b)
            bias_ref[a] = jnp.where(in_window, b, -jnp.inf)

    first = _lane_lt((DBLK, LANES), HEAD_DIM)
    chunk_has_prev = ((j * nblk) % blocks_per_seq) != 0
    for b in range(nblk):
        rows = slice(b * DBLK, (b + 1) * DBLK)
        q = cur_ref[rows, 0:LANES]
        if b == 0:
            kprev, vprev = prev_ref[:, LANES:2 * LANES], prev_ref[:, 2 * LANES:3 * LANES]
        else:
            before = slice((b - 1) * DBLK, b * DBLK)
            kprev, vprev = cur_ref[before, LANES:2 * LANES], cur_ref[before, 2 * LANES:3 * LANES]
        kcat = jnp.concatenate([kprev, cur_ref[rows, LANES:2 * LANES]], axis=0)
        vcat = jnp.concatenate([vprev, cur_ref[rows, 2 * LANES:3 * LANES]], axis=0)
        outs, lses = [], []
        for a in range(2):
            qa = jnp.where(first if a == 0 else jnp.logical_not(first), q, jnp.zeros_like(q))
            s = _dot_nt(qa, kcat) + bias_ref[a]
            if b == 0:
                s = jnp.where(chunk_has_prev | (jj >= DBLK), s, -jnp.inf)
            m = jnp.max(s, axis=-1, keepdims=True)
            e = jnp.exp(s - m)
            den = jnp.sum(e, axis=-1, keepdims=True)
            outs.append(_dot(e.astype(BF16), vcat) / den)
            lses.append(jnp.broadcast_to(m + jnp.log(den), (DBLK, LANES)))
        o_ref[rows, 0:LANES] = jnp.where(first, outs[0], outs[1])
        o_ref[rows, LANES:2 * LANES] = jnp.where(first, lses[0], lses[1])


def _dil(group, dil, table_flat, qkv_sub, bucket):
    s = qkv_sub.shape[0]
    seq = s // dil
    chunk = min(DIL_CHUNK, seq)
    assert seq % chunk == 0 and chunk % DBLK == 0
    nblk = chunk // DBLK
    body = functools.partial(_dil_kernel, group=group, blocks_per_seq=seq // DBLK)
    return pl.pallas_call(
        body,
        grid=(s // chunk,),
        in_specs=[pl.BlockSpec(memory_space=pltpu.SMEM),
                  pl.BlockSpec((chunk, DIL_QKV_W), lambda j: (j, 0)),
                  pl.BlockSpec((DBLK, DIL_QKV_W), lambda j: (jnp.maximum(j * nblk - 1, 0), 0)),
                  pl.BlockSpec((DBLK, 2 * DBLK), lambda j: (0, 0))],
        out_specs=pl.BlockSpec((chunk, 2 * LANES), lambda j: (j, 0)),
        out_shape=jax.ShapeDtypeStruct((s, 2 * LANES), F32),
        scratch_shapes=[pltpu.VMEM((2, DBLK, 2 * DBLK), F32)],
        compiler_params=pltpu.CompilerParams(dimension_semantics=("arbitrary",),
                                             vmem_limit_bytes=VMEM_LIMIT),
        name=f"dil{group}",
    )(table_flat, qkv_sub, qkv_sub, bucket)


def _t5_bucket(dist):
    is_small = dist < T5_MAX_EXACT
    nf = jnp.maximum(dist, T5_MAX_EXACT).astype(F32)
    large = T5_MAX_EXACT + (jnp.log(nf / T5_MAX_EXACT) / np.log(T5_MAX_DISTANCE / T5_MAX_EXACT)
                            * (T5_BUCKETS - T5_MAX_EXACT)).astype(jnp.int32)
    large = jnp.minimum(large, T5_BUCKETS - 1)
    return jnp.where(is_small, dist, large)


def _merge_kernel(x_ref, ofox_ref, od0_ref, od1_ref, od2_ref, omem_ref, wg_ref, wbf_ref, wbd_ref, wbm_ref,
                  wout_ref, g_ref, b_ref, wr_ref, br_ref, h_ref, h3_ref, ri_ref, rg_ref, cnt_ref,
                  carry_ref, tok_ref):
    i = pl.program_id(0)
    tm = x_ref.shape[0]

    @pl.when(i == 0)
    def _():
        carry_ref[...] = jnp.zeros_like(carry_ref)

    x = x_ref[...]
    xb = x.astype(BF16)

    def gate(br):
        return jax.nn.sigmoid(_dot(xb, wg_ref[:, br * D_MODEL:(br + 1) * D_MODEL]))

    merged = gate(0) * _dot(ofox_ref[...], wbf_ref[...])

    for g, (od_ref, (_, dil)) in enumerate(zip((od0_ref, od1_ref, od2_ref), DIL_CONFIGS)):
        for r in range(dil):
            for half in range(2):
                tok_ref[g, half, pl.ds(r, tm // dil, stride=dil), :] = od_ref[r, :, half * LANES:(half + 1) * LANES]
    lse = [tok_ref[g, 1] for g in range(DIL_GROUPS)]
    mx = jnp.maximum(jnp.maximum(lse[0], lse[1]), lse[2])
    ex = [jnp.exp(l - mx) for l in lse]
    den = ex[0] + ex[1] + ex[2]
    bd = None
    for g in range(DIL_GROUPS):
        og = (tok_ref[g, 0] * (ex[g] / den)).astype(BF16)
        t = _dot(og, wbd_ref[g * LANES:(g + 1) * LANES, :])
        bd = t if bd is None else bd + t
    merged = merged + gate(1) * bd
    merged = merged + gate(2) * _dot(omem_ref[...], wbm_ref[...])

    y = _dot(merged.astype(BF16), wout_ref[...])
    h = _layer_norm(DEEPNORM_ALPHA * x + y, g_ref[...], b_ref[...])
    h_ref[...] = h
    _store_token_tiles(h3_ref, h)

    lane = lax.broadcasted_iota(jnp.int32, (tm, LANES), 1)
    logits = _dot(h.astype(BF16), wr_ref[...]) + br_ref[...]
    vals = jnp.where(lane < N_EXPERTS, logits, -jnp.inf)
    tops, hots = [], []
    for _ in range(TOP_K):
        mk = jnp.max(vals, axis=-1, keepdims=True)
        ik = jnp.min(jnp.where(vals == mk, lane, LANES), axis=-1, keepdims=True)
        hot = lane == ik
        vals = jnp.where(hot, -jnp.inf, vals)
        tops.append((mk, ik))
        hots.append(hot)
    es = [jnp.exp(mk - tops[0][0]) for mk, _ in tops]
    esum = es[0] + es[1] + es[2] + es[3]

    picked = jnp.where(hots[0] | hots[1] | hots[2] | hots[3], 1.0, 0.0)
    row = lax.broadcasted_iota(jnp.int32, (tm, tm), 0)
    col = lax.broadcasted_iota(jnp.int32, (tm, tm), 1)
    lower = jnp.where(col < row, 1.0, 0.0).astype(BF16)
    before = _dot(lower, picked.astype(BF16)) + carry_ref[0:1, :]
    ri = jnp.zeros((tm, LANES), jnp.int32)
    rg = jnp.zeros((tm, LANES), F32)
    for k in range(TOP_K):
        rank = jnp.sum(jnp.where(hots[k], before, 0.0), axis=-1, keepdims=True)
        ri = jnp.where(lane == k, tops[k][1], ri)
        ri = jnp.where(lane == TOP_K + k, rank.astype(jnp.int32), ri)
        rg = jnp.where(lane == k, es[k] / esum, rg)
    ri_ref[...] = ri
    rg_ref[...] = rg
    total = carry_ref[...] + jnp.sum(picked, axis=0, keepdims=True)
    carry_ref[...] = total
    cnt_ref[...] = total


def _merge(x2, ofox, ods, omem, wg, wbf, wbd, wbm, wout, g1, b1, wr, br):
    s = x2.shape[0]
    tm = min(MERGE_TM, s)
    full = lambda a: pl.BlockSpec(a.shape, lambda i: (0,) * a.ndim)
    rows = lambda w: pl.BlockSpec((tm, w), lambda i: (i, 0))
    od_specs = [pl.BlockSpec((dil, tm // dil, 2 * LANES), lambda i: (0, i, 0)) for _, dil in DIL_CONFIGS]
    return pl.pallas_call(
        _merge_kernel,
        grid=(s // tm,),
        in_specs=[rows(D_MODEL), rows(FOX_W)] + od_specs +
                 [rows(MEM_W), full(wg), full(wbf), full(wbd), full(wbm), full(wout),
                  full(g1), full(b1), full(wr), full(br)],
        out_specs=[rows(D_MODEL), pl.BlockSpec((tm * SUBLANES, LANES), lambda i: (i, 0)),
                   rows(LANES), rows(LANES), pl.BlockSpec((8, LANES), lambda i: (0, 0))],
        out_shape=[jax.ShapeDtypeStruct((s, D_MODEL), F32),
                   jax.ShapeDtypeStruct((s * SUBLANES, LANES), F32),
                   jax.ShapeDtypeStruct((s, LANES), jnp.int32),
                   jax.ShapeDtypeStruct((s, LANES), F32),
                   jax.ShapeDtypeStruct((8, LANES), F32)],
        scratch_shapes=[pltpu.VMEM((8, LANES), F32), pltpu.VMEM((DIL_GROUPS, 2, tm, LANES), F32)],
        compiler_params=pltpu.CompilerParams(dimension_semantics=("arbitrary",),
                                             vmem_limit_bytes=VMEM_LIMIT),
        name="merge",
    )(x2, ofox, *ods, omem, wg, wbf, wbd, wbm, wout, g1, b1, wr, br)


def _row_copy(src_ref, src_off, dst_ref, dst_off, sem):
    tile = lambda ref, off: ref.at[pl.ds(pl.multiple_of(off, SUBLANES), SUBLANES), :]
    return pltpu.make_async_copy(tile(src_ref, src_off), tile(dst_ref, dst_off), sem)


def _dispatch_kernel(dst_ref, lo_ref, hi_ref, nu_ref, h_ref, xb_ref, zero_ref, hbuf_ref, sem, zsem, lsem, *,
                     s_tiles):
    i = pl.program_id(0)
    n = pl.num_programs(0)
    rows = hbuf_ref.shape[0] // DISPATCH_SLOTS
    tm = rows // SUBLANES
    tile_rows = zero_ref.shape[0]
    n_tiles = xb_ref.shape[0] // tile_rows

    def tail_tile(b):
        return pltpu.make_async_copy(
            zero_ref, xb_ref.at[pl.ds(pl.multiple_of(b * tile_rows, tile_rows), tile_rows), :], zsem)

    def zero_fill(wait):
        def pad_rows(e, _):
            def one(r, _):
                cp = _row_copy(zero_ref, 0, xb_ref, r * SUBLANES, zsem)
                cp.wait() if wait else cp.start()
                return 0
            return lax.fori_loop(lo_ref[e], hi_ref[e], one, 0)
        lax.fori_loop(0, N_EXPERTS, pad_rows, 0)

        def tail(b, _):
            tail_tile(b).wait() if wait else tail_tile(b).start()
            return 0
        lax.fori_loop(nu_ref[0], n_tiles, tail, 0)

    def load(tile, slot):
        return pltpu.make_async_copy(h_ref.at[pl.ds(pl.multiple_of(tile * rows, rows), rows), :],
                                     hbuf_ref.at[pl.ds(pl.multiple_of(slot * rows, rows), rows), :],
                                     lsem.at[slot])

    def wait_scatter(slot):
        def wait(t, _):
            for k in range(TOP_K):
                _row_copy(hbuf_ref, 0, xb_ref, 0, sem.at[slot]).wait()
            return 0
        lax.fori_loop(0, tm, wait, 0, unroll=ROW_DMA_UNROLL)

    slot = i % DISPATCH_SLOTS
    nxt = (i + 1) % DISPATCH_SLOTS

    @pl.when(i == 0)
    def _():
        load(0, 0).start()
        zero_ref[...] = jnp.zeros(zero_ref.shape, F32)
        zero_fill(False)
        zero_fill(True)

    load(i, slot).wait()

    @pl.when(i >= DISPATCH_SLOTS - 1)
    def _():
        wait_scatter(nxt)

    @pl.when(i + 1 < n)
    def _():
        load(i + 1, nxt).start()

    def start(t, _):
        for k in range(TOP_K):
            _row_copy(hbuf_ref, slot * rows + t * SUBLANES, xb_ref, dst_ref[t * TOP_K + k],
                      sem.at[slot]).start(priority=k % 2)
        return 0

    lax.fori_loop(0, tm, start, 0, unroll=ROW_DMA_UNROLL)

    @pl.when(i == n - 1)
    def _():
        for back in range(min(DISPATCH_SLOTS - 1, s_tiles)):
            wait_scatter((i - back) % DISPATCH_SLOTS)


def _dispatch(dst8, pad_lo, pad_hi, n_used, h3, n_tiles, tile_tokens):
    s = h3.shape[0] // SUBLANES
    tm = min(DISPATCH_TM, s)
    smem = pl.BlockSpec(memory_space=pltpu.SMEM)
    body = functools.partial(_dispatch_kernel, s_tiles=s // tm)
    return pl.pallas_call(
        body,
        grid=(s // tm,),
        in_specs=[pl.BlockSpec((tm * TOP_K,), lambda i: (i,), memory_space=pltpu.SMEM), smem, smem, smem,
                  pl.BlockSpec(memory_space=pl.ANY)],
        out_specs=pl.BlockSpec(memory_space=pl.ANY),
        out_shape=jax.ShapeDtypeStruct((n_tiles * tile_tokens * SUBLANES, LANES), F32),
        scratch_shapes=[pltpu.VMEM((tile_tokens * SUBLANES, LANES), F32),
                        pltpu.VMEM((DISPATCH_SLOTS * tm * SUBLANES, LANES), F32),
                        pltpu.SemaphoreType.DMA((DISPATCH_SLOTS,)), pltpu.SemaphoreType.DMA(()),
                        pltpu.SemaphoreType.DMA((DISPATCH_SLOTS,))],
        compiler_params=pltpu.CompilerParams(dimension_semantics=("arbitrary",),
                                             vmem_limit_bytes=VMEM_LIMIT),
        name="dispatch",
    )(dst8, pad_lo, pad_hi, n_used, h3)


def _expert_kernel(te_ref, nu_ref, x_ref, w1_ref, b1_ref, w2_ref, b2_ref, y_ref, w1b_ref, w2b_ref):
    b = pl.program_id(0)

    @pl.when(b >= nu_ref[0])
    def _():
        y_ref[...] = jnp.zeros(y_ref.shape, y_ref.dtype)

    @pl.when(b < nu_ref[0])
    def _():
        prev = te_ref[jnp.maximum(b - 1, 0)]

        @pl.when((b == 0) | (te_ref[b] != prev))
        def _():
            chunk = 128
            def cast(r, _):
                rows = pl.ds(pl.multiple_of(r * chunk, chunk), chunk)
                w1b_ref[rows, :] = w1_ref[rows, :].astype(BF16)
                w2b_ref[rows, :] = w2_ref[rows, :].astype(BF16)
                return 0
            lax.fori_loop(0, D_MODEL // chunk, cast, 0)

        xe = _load_token_tiles(x_ref, 0, x_ref.shape[0] // SUBLANES).astype(BF16)
        hcat = _dot(xe, w1b_ref[...]) + b1_ref[...]
        h_glu = jnp.minimum(hcat[:, :D_FF], SWIGLU_LIMIT)
        h_lin = jnp.clip(hcat[:, D_FF:], -SWIGLU_LIMIT, SWIGLU_LIMIT)
        act = h_glu * jax.nn.sigmoid(SWIGLU_ALPHA * h_glu) * (h_lin + 1.0)
        _store_token_tiles(y_ref, _dot(act.astype(BF16), w2b_ref[...]) + b2_ref[...])


def _experts(tile_e, n_used, xb, w1, b1, w2, b2):
    p = xb.shape[0] // SUBLANES
    tm = EXPERT_TM
    row_map = lambda b, te, nu: (jnp.minimum(b, nu[0] - 1), 0)
    exp_map = lambda b, te, nu: (te[b], 0, 0)
    return pl.pallas_call(
        _expert_kernel,
        grid_spec=pltpu.PrefetchScalarGridSpec(
            num_scalar_prefetch=2,
            grid=(p // tm,),
            in_specs=[pl.BlockSpec((tm * SUBLANES, LANES), row_map),
                      pl.BlockSpec((None, D_MODEL, 2 * D_FF), exp_map),
                      pl.BlockSpec((None, 1, 2 * D_FF), exp_map),
                      pl.BlockSpec((None, D_FF, D_MODEL), exp_map),
                      pl.BlockSpec((None, 1, D_MODEL), exp_map)],
            out_specs=pl.BlockSpec((tm * SUBLANES, LANES), lambda b, te, nu: (b, 0)),
            scratch_shapes=[pltpu.VMEM((D_MODEL, 2 * D_FF), BF16), pltpu.VMEM((D_FF, D_MODEL), BF16)]),
        out_shape=jax.ShapeDtypeStruct((p * SUBLANES, LANES), F32),
        compiler_params=pltpu.CompilerParams(dimension_semantics=("arbitrary",),
                                             vmem_limit_bytes=VMEM_LIMIT),
        name="experts",
    )(tile_e, n_used, xb, w1, b1, w2, b2)


def _combine_kernel(cur_ref, nxt_ref, h_ref, rg_ref, g_ref, b_ref, yb_ref, o_ref, buf_ref, sem):
    i = pl.program_id(0)
    tm = h_ref.shape[0]
    slot_rows = TOP_K * tm * SUBLANES
    slot = i % 2

    def gather(src_ref, into):
        def start(t, _):
            for k in range(TOP_K):
                _row_copy(yb_ref, src_ref[t * TOP_K + k], buf_ref, into * slot_rows + (k * tm + t) * SUBLANES,
                          sem.at[into]).start(priority=k % 2)
            return 0
        lax.fori_loop(0, tm, start, 0, unroll=ROW_DMA_UNROLL)

    @pl.when(i == 0)
    def _():
        gather(cur_ref, 0)

    def wait(t, _):
        for k in range(TOP_K):
            _row_copy(yb_ref, 0, buf_ref, 0, sem.at[slot]).wait()
        return 0

    lax.fori_loop(0, tm, wait, 0, unroll=ROW_DMA_UNROLL)

    @pl.when(i + 1 < pl.num_programs(0))
    def _():
        gather(nxt_ref, 1 - slot)

    h = h_ref[...]
    rg = rg_ref[...]
    moe = None
    for k in range(TOP_K):
        t = rg[:, k:k + 1] * _load_token_tiles(buf_ref, slot * (TOP_K * tm) + k * tm, tm)
        moe = t if moe is None else moe + t
    o_ref[...] = _layer_norm(DEEPNORM_ALPHA * h + moe, g_ref[...], b_ref[...])


def _combine(dst8, h, rg, g2, b2, yb):
    s = h.shape[0]
    tm = min(COMBINE_TM, s)
    last = s // tm - 1
    full = lambda a: pl.BlockSpec(a.shape, lambda i: (0,) * a.ndim)
    return pl.pallas_call(
        _combine_kernel,
        grid=(s // tm,),
        in_specs=[pl.BlockSpec((tm * TOP_K,), lambda i: (i,), memory_space=pltpu.SMEM),
                  pl.BlockSpec((tm * TOP_K,), lambda i: (jnp.minimum(i + 1, last),), memory_space=pltpu.SMEM),
                  pl.BlockSpec((tm, D_MODEL), lambda i: (i, 0)),
                  pl.BlockSpec((tm, LANES), lambda i: (i, 0)),
                  full(g2), full(b2),
                  pl.BlockSpec(memory_space=pl.ANY)],
        out_specs=pl.BlockSpec((tm, D_MODEL), lambda i: (i, 0)),
        out_shape=jax.ShapeDtypeStruct((s, D_MODEL), F32),
        scratch_shapes=[pltpu.VMEM((2 * TOP_K * tm * SUBLANES, LANES), F32), pltpu.SemaphoreType.DMA((2,))],
        compiler_params=pltpu.CompilerParams(dimension_semantics=("arbitrary",),
                                             vmem_limit_bytes=VMEM_LIMIT),
        name="combine",
    )(dst8, dst8, h, rg, g2, b2, yb)


def kernel(x, mem, w_in, b_fgate, t5_bias, w_mem_kv, w_br_fox, w_br_dil, w_br_mem, w_out, ln1_g, ln1_b,
           w_router, b_router, w_exp_in, b_exp_in, w_exp_out, b_exp_out, ln2_g, ln2_b):
    bsz, s, d = x.shape
    assert bsz == 1 and d == D_MODEL and w_in.shape[0] == 1
    assert s % (DIL_CONFIGS[-1][1] * DBLK) == 0
    x2 = x[0]
    w_in0 = w_in[0]

    fg_lanes = np.array([FG_PAIR_STRIDE * (h // 2) + FG_PIECES * (h % 2) + j
                         for h in range(FOX_HEADS) for j in range(FG_PIECES)])
    fg_heads = np.repeat(np.arange(FOX_HEADS), FG_PIECES)
    w_fg = jnp.zeros((D_MODEL, LANES), F32).at[:, fg_lanes].set(w_in0[:, OFF_FOX_F + fg_heads])
    bf = jnp.zeros((1, LANES), F32).at[0, fg_lanes].set(b_fgate[0][fg_heads])
    w_all = jnp.concatenate([w_in0[:, OFF_FOX_QKV:OFF_FOX_QKV + 2 * FOX_W], w_in0[:, OFF_DIL_QKV:OFF_GATES],
                             w_fg], axis=1).astype(BF16)
    wvt = w_in0[:, OFF_FOX_QKV + 2 * FOX_W:OFF_FOX_F].T.astype(BF16)
    w_gates = w_in0[:, OFF_GATES:].astype(BF16)

    qk, ext, vt, o_mem, *dil_qkv, stats = _proj(x2, w_all, wvt, bf, mem[0].astype(BF16),
                                                w_mem_kv[0].astype(BF16))

    tq = min(FOX_TQ, s)
    assert min(PROJ_TM, s) == min(FOX_TK, tq)
    nblk = stats.shape[0]
    gate_lane = np.array([FG_PAIR_STRIDE * (h // 2) + FG_PIECES * (h % 2) for h in range(FOX_HEADS)])
    q_norm = jnp.sqrt(stats[:, 0, :FOX_HEADS, 0])
    k_norm = lax.cummax(jnp.sqrt(stats[:, 1, :FOX_HEADS, 0]), axis=0)
    neg_c = lax.cummax(stats[:, 2, 0, gate_lane], axis=0)
    q_norm = jnp.max(q_norm.reshape(s // tq, nblk * tq // s, FOX_HEADS), axis=1)
    o_fox = _fox(k_norm.T.reshape(-1), neg_c.T.reshape(-1), q_norm.T.reshape(-1), qk, ext, vt)

    ii = np.arange(DBLK, dtype=np.int32)[:, None]
    jj = np.arange(2 * DBLK, dtype=np.int32)[None, :]
    rel = np.clip(DBLK + ii - jj, 0, None)
    table_flat = t5_bias.reshape(-1)
    ods = []
    for g, (window, dil) in enumerate(DIL_CONFIGS):
        assert window // dil == DBLK
        bucket = _t5_bucket(jnp.asarray(rel * dil, dtype=jnp.int32))
        od = _dil(g, dil, table_flat, dil_qkv[g].reshape(s, DIL_QKV_W), bucket)
        ods.append(od.reshape(dil, s // dil, 2 * LANES))

    wr = jnp.zeros((D_MODEL, LANES), F32).at[:, :N_EXPERTS].set(w_router[0]).astype(BF16)
    br = jnp.zeros((1, LANES), F32).at[0, :N_EXPERTS].set(b_router[0])
    h1, h3, ri, rg, cnt = _merge(x2, o_fox, ods, o_mem, w_gates, w_br_fox[0].astype(BF16),
                                 w_br_dil[0].astype(BF16), w_br_mem[0].astype(BF16), w_out[0].astype(BF16),
                                 ln1_g, ln1_b, wr, br)

    tm = EXPERT_TM
    counts = cnt[0, :N_EXPERTS].astype(jnp.int32)
    padded = (counts + tm - 1) // tm * tm
    pad_ends = jnp.cumsum(padded)
    pad_starts = (pad_ends - padded).astype(jnp.int32)
    n_tiles = (s * TOP_K) // tm + N_EXPERTS
    n_used = (pad_ends[-1] // tm).astype(jnp.int32)
    tile_lo = jnp.arange(n_tiles, dtype=jnp.int32) * tm
    tile_e = jnp.minimum(jnp.sum((pad_ends[None, :] <= tile_lo[:, None]).astype(jnp.int32), axis=1),
                         N_EXPERTS - 1)
    tile_e = tile_e[jnp.minimum(jnp.arange(n_tiles), n_used - 1)]
    e_tk = ri[:, 0:TOP_K]
    start_tk = jnp.sum(jnp.where(e_tk[:, :, None] == jnp.arange(N_EXPERTS, dtype=jnp.int32),
                                 pad_starts, 0), axis=-1)
    dst8 = ((start_tk + ri[:, TOP_K:2 * TOP_K]) * SUBLANES).reshape(-1)

    xb = _dispatch(dst8, pad_starts + counts, pad_ends.astype(jnp.int32), n_used.reshape(1), h3, n_tiles, tm)
    yb = _experts(tile_e, n_used.reshape(1), xb, w_exp_in[0], b_exp_in[0][:, None, :], w_exp_out[0],
                  b_exp_out[0][:, None, :])
    out = _combine(dst8, h1, rg, ln2_g, ln2_b, yb)
    return out[None]
```

```python
import functools

import numpy as np
import jax
import jax.numpy as jnp
from jax import lax
from jax.experimental import pallas as pl
from jax.experimental.pallas import tpu as pltpu

D_MODEL = 1024
HEAD_DIM = 64
LANES = 128
SUBLANES = 8
assert D_MODEL == SUBLANES * LANES
FOX_HEADS = 6
FOX_PAIRS = FOX_HEADS // 2
DIL_CONFIGS = ((128, 1), (512, 4), (2048, 16))
DIL_GROUPS = len(DIL_CONFIGS)
DIL_HEADS = 2 * DIL_GROUPS
MEM_HEADS = 4
MEM_PAIRS = MEM_HEADS // 2
MEM_LEN = 256
FOX_W = FOX_HEADS * HEAD_DIM
DIL_W = DIL_HEADS * HEAD_DIM
MEM_W = MEM_HEADS * HEAD_DIM
DBLK = 128
DIL_QKV_W = 3 * LANES
T5_BUCKETS = 32
T5_MAX_EXACT = T5_BUCKETS // 2
T5_MAX_DISTANCE = 2048
N_EXPERTS = 32
TOP_K = 4
D_FF = D_MODEL
SWIGLU_LIMIT = 7.0
SWIGLU_ALPHA = 1.702
LN_EPS = 1e-5
DEEPNORM_ALPHA = 2.0 ** 0.25
QK_SCALE = HEAD_DIM ** -0.5
F32_EXP_ZERO_BELOW = 110.0

OFF_FOX_QKV = 0
OFF_FOX_F = OFF_FOX_QKV + 3 * FOX_W
OFF_DIL_QKV = OFF_FOX_F + FOX_HEADS
OFF_MEM_Q = OFF_DIL_QKV + 3 * DIL_W
OFF_GATES = OFF_MEM_Q + MEM_W

W_FOXQ = 0
W_FOXK = FOX_W
W_DIL = 2 * FOX_W
W_MEMQ = W_DIL + 3 * DIL_W
W_FGATE = W_MEMQ + MEM_W
W_ALL = W_FGATE + LANES
FG_PAIR_STRIDE = 8
FG_PIECES = 3
VT_ROWS = HEAD_DIM + 16

VMEM_LIMIT = 56 * 1024 * 1024

PROJ_TM = 512
FOX_TQ = 1024
FOX_TK = 512
FOX_TC = 1024
DIL_CHUNK = 512
MERGE_TM = 512
DISPATCH_TM = 256
DISPATCH_SLOTS = 3
EXPERT_TM = 512
COMBINE_TM = 256
ROW_DMA_UNROLL = 4

BF16 = jnp.bfloat16
F32 = jnp.float32


def _dot(a, b):
    return jnp.dot(a, b, preferred_element_type=F32)


def _dot_nt(a, b):
    return lax.dot_general(a, b, (((1,), (1,)), ((), ())), preferred_element_type=F32)


def _lane_lt(shape, n):
    return lax.broadcasted_iota(jnp.int32, shape, len(shape) - 1) < n


def _split3(v):
    hi = v.astype(BF16)
    r1 = v - hi.astype(F32)
    mid = r1.astype(BF16)
    lo = (r1 - mid.astype(F32)).astype(BF16)
    return hi, mid, lo


def _store_token_tiles(ref, v):
    n = v.shape[0]
    for c in range(SUBLANES):
        ref[pl.ds(c, n, stride=SUBLANES), :] = v[:, c * LANES:(c + 1) * LANES]


def _load_token_tiles(ref, lo, n):
    return jnp.concatenate([ref[pl.ds(lo * SUBLANES + c, n, stride=SUBLANES), :] for c in range(SUBLANES)],
                           axis=1)


def _layer_norm(r, g, b):
    mu = jnp.mean(r, axis=-1, keepdims=True)
    d = r - mu
    var = jnp.mean(d * d, axis=-1, keepdims=True)
    return d * lax.rsqrt(var + LN_EPS) * g + b


def _proj_kernel(x_ref, w_ref, wvt_ref, bf_ref, mem_ref, wkv_ref, qk_ref, ext_ref, vt_ref, omem_ref,
                 d0_ref, d1_ref, d2_ref, stat_ref, kvm_ref, carry_ref, stage_ref):
    i = pl.program_id(0)
    tm = x_ref.shape[0]

    @pl.when(i == 0)
    def _():
        kvm_ref[...] = _dot(mem_ref[...], wkv_ref[...]).astype(BF16)
        carry_ref[...] = jnp.zeros_like(carry_ref)

    xb = x_ref[...].astype(BF16)
    proj = _dot(xb, w_ref[...])

    qb = (proj[:, W_FOXQ:W_FOXK] * QK_SCALE).astype(BF16)
    kb = proj[:, W_FOXK:W_DIL].astype(BF16)
    qk_ref[:, W_FOXQ:W_FOXK] = qb
    qk_ref[:, W_FOXK:W_DIL] = kb

    stat_ref[...] = jnp.zeros(stat_ref.shape, F32)
    first_head = _lane_lt((tm, LANES), HEAD_DIM)
    for which, vb in enumerate((qb, kb)):
        for p in range(FOX_PAIRS):
            sq = jnp.square(vb[:, p * LANES:(p + 1) * LANES].astype(F32))
            for a in range(2):
                mask = first_head if a == 0 else jnp.logical_not(first_head)
                norm2 = jnp.sum(jnp.where(mask, sq, 0.0), axis=1, keepdims=True)
                top = jnp.max(norm2, axis=0, keepdims=True)
                stat_ref[0, which, 2 * p + a:2 * p + a + 1, :] = jnp.broadcast_to(top, (1, LANES))

    for g, (d_ref, (_, dil)) in enumerate(zip((d0_ref, d1_ref, d2_ref), DIL_CONFIGS)):
        cols = [proj[:, W_DIL + part * DIL_W + g * LANES:W_DIL + part * DIL_W + (g + 1) * LANES]
                for part in range(3)]
        cols[0] = cols[0] * QK_SCALE
        if dil == 1:
            for part in range(3):
                d_ref[0, :, part * LANES:(part + 1) * LANES] = cols[part].astype(BF16)
        else:
            for part in range(3):
                stage_ref[part] = cols[part]
            for r in range(dil):
                for part in range(3):
                    d_ref[r, :, part * LANES:(part + 1) * LANES] = (
                        stage_ref[part, pl.ds(r, tm // dil, stride=dil), :].astype(BF16))

    vt = _dot_nt(wvt_ref[...], xb)
    for h in range(FOX_HEADS):
        vt_ref[h, 0:HEAD_DIM, :] = vt[h * HEAD_DIM:(h + 1) * HEAD_DIM, :].astype(BF16)
        vt_ref[h, HEAD_DIM:VT_ROWS, :] = jnp.ones((VT_ROWS - HEAD_DIM, tm), BF16)

    z = proj[:, W_FGATE:W_ALL] + bf_ref[...]
    logf = jnp.minimum(z, 0.0) - jnp.log1p(jnp.exp(-jnp.abs(z)))
    row = lax.broadcasted_iota(jnp.int32, (tm, tm), 0)
    col = lax.broadcasted_iota(jnp.int32, (tm, tm), 1)
    lower = jnp.where(col <= row, 1.0, 0.0).astype(BF16)
    hi, mid, lo = _split3(logf)
    c = (_dot(lower, lo) + _dot(lower, mid)) + _dot(lower, hi) + carry_ref[0:1, :]
    carry_ref[...] = jnp.broadcast_to(c[tm - 1:tm, :], carry_ref.shape)
    stat_ref[0, 2, 0:1, :] = jnp.max(-c, axis=0, keepdims=True)
    nhi, nmid, nlo = _split3(-c)
    lane = lax.broadcasted_iota(jnp.int32, (tm, LANES), 1)
    slot = lane & (FG_PAIR_STRIDE - 1)
    piece = jnp.where(slot >= FG_PIECES, slot - FG_PIECES, slot)
    used = (slot < 2 * FG_PIECES) & (lane < FOX_PAIRS * FG_PAIR_STRIDE)
    ext = jnp.where(piece == 0, nhi, jnp.where(piece == 1, nmid, nlo))
    ext_ref[...] = jnp.where(used, ext, jnp.zeros_like(ext))

    qm = (proj[:, W_MEMQ:W_FGATE] * QK_SCALE).astype(BF16)
    for p in range(MEM_PAIRS):
        qp = qm[:, p * LANES:(p + 1) * LANES]
        kp = kvm_ref[:, p * LANES:(p + 1) * LANES]
        vp = kvm_ref[:, MEM_W + p * LANES:MEM_W + (p + 1) * LANES]
        first = _lane_lt(qp.shape, HEAD_DIM)
        outs = []
        for a in range(2):
            qa = jnp.where(first if a == 0 else jnp.logical_not(first), qp, jnp.zeros_like(qp))
            s = _dot_nt(qa, kp)
            m = jnp.max(s, axis=-1, keepdims=True)
            e = jnp.exp(s - m)
            den = jnp.sum(e, axis=-1, keepdims=True)
            outs.append(_dot(e.astype(BF16), vp) / den)
        omem_ref[:, p * LANES:(p + 1) * LANES] = jnp.where(first, outs[0], outs[1]).astype(BF16)


def _proj(x2, w_all, wvt, bf, mem_b, wkv_b):
    s = x2.shape[0]
    tm = min(PROJ_TM, s)
    full = lambda shape: pl.BlockSpec(shape, lambda i: (0,) * len(shape))
    rows = lambda w: pl.BlockSpec((tm, w), lambda i: (i, 0))
    dil_spec = lambda dil: pl.BlockSpec((dil, tm // dil, DIL_QKV_W), lambda i: (0, i, 0))
    dil_shape = lambda dil: jax.ShapeDtypeStruct((dil, s // dil, DIL_QKV_W), BF16)
    dils = [dil for _, dil in DIL_CONFIGS]
    return pl.pallas_call(
        _proj_kernel,
        grid=(s // tm,),
        in_specs=[rows(D_MODEL),
                  full(w_all.shape), full(wvt.shape), full(bf.shape), full(mem_b.shape), full(wkv_b.shape)],
        out_specs=[rows(W_DIL), rows(LANES),
                   pl.BlockSpec((FOX_HEADS, VT_ROWS, tm), lambda i: (0, 0, i)),
                   rows(MEM_W)] + [dil_spec(d) for d in dils] +
                  [pl.BlockSpec((1, 3, SUBLANES, LANES), lambda i: (i, 0, 0, 0))],
        out_shape=[jax.ShapeDtypeStruct((s, W_DIL), BF16),
                   jax.ShapeDtypeStruct((s, LANES), BF16),
                   jax.ShapeDtypeStruct((FOX_HEADS, VT_ROWS, s), BF16),
                   jax.ShapeDtypeStruct((s, MEM_W), BF16)] + [dil_shape(d) for d in dils] +
                  [jax.ShapeDtypeStruct((s // tm, 3, SUBLANES, LANES), F32)],
        scratch_shapes=[pltpu.VMEM((MEM_LEN, 2 * MEM_W), BF16), pltpu.VMEM((8, LANES), F32),
                        pltpu.VMEM((3, tm, LANES), F32)],
        compiler_params=pltpu.CompilerParams(dimension_semantics=("arbitrary",),
                                             vmem_limit_bytes=VMEM_LIMIT),
        name="proj",
    )(x2, w_all, wvt, bf, mem_b, wkv_b)


def _fox_kernel(km_ref, nc_ref, qn_ref, q_ref, k_ref, e_ref, vt_ref, o_ref, m_ref, acc_ref, qx_ref, s_ref):
    p = pl.program_id(0)
    i = pl.program_id(1)
    t = q_ref.shape[0]
    tk = min(FOX_TK, t)
    q = q_ref[...]
    lane = lax.broadcasted_iota(jnp.int32, (t, LANES), 1)
    for a in range(2):
        head = (lane < HEAD_DIM) if a == 0 else (lane >= HEAD_DIM)
        lo = FG_PAIR_STRIDE * p + FG_PIECES * a
        ones = jnp.where((lane >= lo) & (lane < lo + FG_PIECES), 1.0, 0.0).astype(BF16)
        qx_ref[a, :, 0:LANES] = jnp.where(head, q, jnp.zeros_like(q))
        qx_ref[a, :, LANES:2 * LANES] = ones

    m_ref[...] = jnp.full(m_ref.shape, -jnp.inf, F32)
    acc_ref[...] = jnp.zeros(acc_ref.shape, F32)

    tc = min(FOX_TC, t)

    def phase(nxt, cur):
        if nxt is not None:
            nblk, nq0, nbuf = nxt
            nstart = pl.multiple_of(nblk * tk, tk)
            kx = jnp.concatenate([k_ref[pl.ds(nstart, tk), :], e_ref[pl.ds(nstart, tk), :]], axis=1)
        if cur is not None:
            cblk, cq0, cbuf, diagonal = cur
            cstart = pl.multiple_of(cblk * tk, tk)
            vtb = [vt_ref[a, :, pl.ds(cstart, tk)] for a in range(2)]
        for c in range(0, t, tc):
            if nxt is not None and c + tc > nq0:
                cols = slice(max(c, nq0), c + tc)
                for a in range(2):
                    s_ref[nbuf, a, :, cols] = _dot_nt(kx, qx_ref[a, cols, :])
            if cur is not None and c + tc > cq0:
                cols = slice(max(c, cq0), c + tc)
                for a in range(2):
                    s = s_ref[cbuf, a, :, cols]
                    if diagonal:
                        kpos = lax.broadcasted_iota(jnp.int32, s.shape, 0) + cq0
                        qpos = lax.broadcasted_iota(jnp.int32, s.shape, 1) + cols.start
                        s = jnp.where(kpos <= qpos, s, -jnp.inf)
                    m_prev = m_ref[a, :, cols]
                    m_new = jnp.maximum(m_prev, jnp.max(s, axis=0, keepdims=True))
                    alpha = jnp.exp(m_prev - m_new)
                    pt = jnp.exp(s - m_new).astype(BF16)
                    m_ref[a, :, cols] = m_new
                    acc_ref[a, :, cols] = alpha * acc_ref[a, :, cols] + _dot(vtb[a], pt)

    nd = t // tk
    nfull = i * nd
    phase((nfull, 0, 0), None)
    for d in range(nd):
        nxt = (nfull + d + 1, (d + 1) * tk, (d + 1) % 2) if d + 1 < nd else None
        phase(nxt, (nfull + d, d * tk, d % 2, True))

    nblocks = k_ref.shape[0] // tk
    nq = pl.num_programs(1)
    need = jnp.int32(0)
    for a in range(2):
        h = 2 * p + a
        thr = jnp.min(m_ref[a]) - F32_EXP_ZERO_BELOW
        qn = qn_ref[h * nq + i]
        count = lax.fori_loop(
            0, nfull,
            lambda j, n: n + (qn * km_ref[h * nblocks + j] + nc_ref[h * nblocks + j] >= thr).astype(jnp.int32),
            jnp.int32(0))
        need = jnp.maximum(need, count)
    need = jnp.minimum(jnp.bitwise_and(need + 1, -2), nfull)
    first = nfull - need

    @pl.when(need > 0)
    def _():
        phase((first, 0, 0), None)

        def pair(j, _):
            b0 = first + 2 * j
            phase((b0 + 1, 0, 1), (b0, 0, 0, False))
            phase((b0 + 2, 0, 0), (b0 + 1, 0, 1, False))
            return 0

        lax.fori_loop(0, need // 2 - 1, pair, 0)
        phase((nfull - 1, 0, 1), (nfull - 2, 0, 0, False))
        phase(None, (nfull - 1, 0, 1, False))
    o0 = acc_ref[0, 0:HEAD_DIM, :] / acc_ref[0, HEAD_DIM:HEAD_DIM + 1, :]
    o1 = acc_ref[1, 0:HEAD_DIM, :] / acc_ref[1, HEAD_DIM:HEAD_DIM + 1, :]
    o_ref[...] = jnp.concatenate([o0, o1], axis=0).T.astype(BF16)


def _fox(km, nc, qn, qk, ext, vt):
    s = qk.shape[0]
    t = min(FOX_TQ, s)
    tk = min(FOX_TK, t)
    assert s % t == 0 and t % (2 * tk) == 0
    kcol = W_FOXK // LANES
    return pl.pallas_call(
        _fox_kernel,
        grid=(FOX_PAIRS, s // t),
        in_specs=[pl.BlockSpec(memory_space=pltpu.SMEM), pl.BlockSpec(memory_space=pltpu.SMEM),
                  pl.BlockSpec(memory_space=pltpu.SMEM),
                  pl.BlockSpec((t, LANES), lambda p, i: (i, p)),
                  pl.BlockSpec((s, LANES), lambda p, i: (0, kcol + p)),
                  pl.BlockSpec((s, LANES), lambda p, i: (0, 0)),
                  pl.BlockSpec((2, VT_ROWS, s), lambda p, i: (p, 0, 0))],
        out_specs=pl.BlockSpec((t, LANES), lambda p, i: (i, p)),
        out_shape=jax.ShapeDtypeStruct((s, FOX_W), BF16),
        scratch_shapes=[pltpu.VMEM((2, 1, t), F32), pltpu.VMEM((2, VT_ROWS, t), F32),
                        pltpu.VMEM((2, t, 2 * LANES), BF16), pltpu.VMEM((2, 2, tk, t), F32)],
        compiler_params=pltpu.CompilerParams(dimension_semantics=("arbitrary", "arbitrary"),
                                             vmem_limit_bytes=VMEM_LIMIT),
        name="fox",
    )(km, nc, qn, qk, qk, ext, vt)


def _dil_kernel(table_ref, cur_ref, prev_ref, bucket_ref, o_ref, bias_ref, *, group, blocks_per_seq):
    j = pl.program_id(0)
    nblk = cur_ref.shape[0] // DBLK
    ii = lax.broadcasted_iota(jnp.int32, (DBLK, 2 * DBLK), 0)
    jj = lax.broadcasted_iota(jnp.int32, (DBLK, 2 * DBLK), 1)

    @pl.when(j == 0)
    def _():
        bucket = bucket_ref[...]
        rel = DBLK + ii - jj
        in_window = (rel >= 0) & (rel <= DBLK)
        for a in range(2):
            b = jnp.zeros(bucket.shape, F32)
            for t in range(T5_BUCKETS):
                b = jnp.where(bucket == t, table_ref[t * DIL_HEADS + 2 * group + a], b)
            bias_ref[a] = jnp.where(in_window, b, -jnp.inf)

    first = _lane_lt((DBLK, LANES), HEAD_DIM)
    chunk_has_prev = ((j * nblk) % blocks_per_seq) != 0
    for b in range(nblk):
        rows = slice(b * DBLK, (b + 1) * DBLK)
        q = cur_ref[rows, 0:LANES]
        if b == 0:
            kprev, vprev = prev_ref[:, LANES:2 * LANES], prev_ref[:, 2 * LANES:3 * LANES]
        else:
            before = slice((b - 1) * DBLK, b * DBLK)
            kprev, vprev = cur_ref[before, LANES:2 * LANES], cur_ref[before, 2 * LANES:3 * LANES]
        kcat = jnp.concatenate([kprev, cur_ref[rows, LANES:2 * LANES]], axis=0)
        vcat = jnp.concatenate([vprev, cur_ref[rows, 2 * LANES:3 * LANES]], axis=0)
        outs, lses = [], []
        for a in range(2):
            qa = jnp.where(first if a == 0 else jnp.logical_not(first), q, jnp.zeros_like(q))
            s = _dot_nt(qa, kcat) + bias_ref[a]
            if b == 0:
                s = jnp.where(chunk_has_prev | (jj >= DBLK), s, -jnp.inf)
            m = jnp.max(s, axis=-1, keepdims=True)
            e = jnp.exp(s - m)
            den = jnp.sum(e, axis=-1, keepdims=True)
            outs.append(_dot(e.astype(BF16), vcat) / den)
            lses.append(jnp.broadcast_to(m + jnp.log(den), (DBLK, LANES)))
        o_ref[rows, 0:LANES] = jnp.where(first, outs[0], outs[1])
        o_ref[rows, LANES:2 * LANES] = jnp.where(first, lses[0], lses[1])


def _dil(group, dil, table_flat, qkv_sub, bucket):
    s = qkv_sub.shape[0]
    seq = s // dil
    chunk = min(DIL_CHUNK, seq)
    assert seq % chunk == 0 and chunk % DBLK == 0
    nblk = chunk // DBLK
    body = functools.partial(_dil_kernel, group=group, blocks_per_seq=seq // DBLK)
    return pl.pallas_call(
        body,
        grid=(s // chunk,),
        in_specs=[pl.BlockSpec(memory_space=pltpu.SMEM),
                  pl.BlockSpec((chunk, DIL_QKV_W), lambda j: (j, 0)),
                  pl.BlockSpec((DBLK, DIL_QKV_W), lambda j: (jnp.maximum(j * nblk - 1, 0), 0)),
                  pl.BlockSpec((DBLK, 2 * DBLK), lambda j: (0, 0))],
        out_specs=pl.BlockSpec((chunk, 2 * LANES), lambda j: (j, 0)),
        out_shape=jax.ShapeDtypeStruct((s, 2 * LANES), F32),
        scratch_shapes=[pltpu.VMEM((2, DBLK, 2 * DBLK), F32)],
        compiler_params=pltpu.CompilerParams(dimension_semantics=("arbitrary",),
                                             vmem_limit_bytes=VMEM_LIMIT),
        name=f"dil{group}",
    )(table_flat, qkv_sub, qkv_sub, bucket)


def _t5_bucket(dist):
    is_small = dist < T5_MAX_EXACT
    nf = jnp.maximum(dist, T5_MAX_EXACT).astype(F32)
    large = T5_MAX_EXACT + (jnp.log(nf / T5_MAX_EXACT) / np.log(T5_MAX_DISTANCE / T5_MAX_EXACT)
                            * (T5_BUCKETS - T5_MAX_EXACT)).astype(jnp.int32)
    large = jnp.minimum(large, T5_BUCKETS - 1)
    return jnp.where(is_small, dist, large)


def _merge_kernel(x_ref, ofox_ref, od0_ref, od1_ref, od2_ref, omem_ref, wg_ref, wbf_ref, wbd_ref, wbm_ref,
                  wout_ref, g_ref, b_ref, wr_ref, br_ref, h_ref, h3_ref, ri_ref, rg_ref, cnt_ref,
                  carry_ref, tok_ref):
    i = pl.program_id(0)
    tm = x_ref.shape[0]

    @pl.when(i == 0)
    def _():
        carry_ref[...] = jnp.zeros_like(carry_ref)

    x = x_ref[...]
    xb = x.astype(BF16)

    def gate(br):
        return jax.nn.sigmoid(_dot(xb, wg_ref[:, br * D_MODEL:(br + 1) * D_MODEL]))

    merged = gate(0) * _dot(ofox_ref[...], wbf_ref[...])

    for g, (od_ref, (_, dil)) in enumerate(zip((od0_ref, od1_ref, od2_ref), DIL_CONFIGS)):
        for r in range(dil):
            for half in range(2):
                tok_ref[g, half, pl.ds(r, tm // dil, stride=dil), :] = od_ref[r, :, half * LANES:(half + 1) * LANES]
    lse = [tok_ref[g, 1] for g in range(DIL_GROUPS)]
    mx = jnp.maximum(jnp.maximum(lse[0], lse[1]), lse[2])
    ex = [jnp.exp(l - mx) for l in lse]
    den = ex[0] + ex[1] + ex[2]
    o_dil = jnp.concatenate([(tok_ref[g, 0] * (ex[g] / den)).astype(BF16) for g in range(DIL_GROUPS)], axis=1)
    merged = merged + gate(1) * _dot(o_dil, wbd_ref[...])
    merged = merged + gate(2) * _dot(omem_ref[...], wbm_ref[...])

    y = _dot(merged.astype(BF16), wout_ref[...])
    h = _layer_norm(DEEPNORM_ALPHA * x + y, g_ref[...], b_ref[...])
    h_ref[...] = h
    _store_token_tiles(h3_ref, h)

    lane = lax.broadcasted_iota(jnp.int32, (tm, LANES), 1)
    logits = _dot(h.astype(BF16), wr_ref[...]) + br_ref[...]
    vals = jnp.where(lane < N_EXPERTS, logits, -jnp.inf)
    tops, hots = [], []
    for _ in range(TOP_K):
        mk = jnp.max(vals, axis=-1, keepdims=True)
        ik = jnp.min(jnp.where(vals == mk, lane, LANES), axis=-1, keepdims=True)
        hot = lane == ik
        vals = jnp.where(hot, -jnp.inf, vals)
        tops.append((mk, ik))
        hots.append(hot)
    es = [jnp.exp(mk - tops[0][0]) for mk, _ in tops]
    esum = es[0] + es[1] + es[2] + es[3]

    picked = jnp.where(hots[0] | hots[1] | hots[2] | hots[3], 1.0, 0.0)
    row = lax.broadcasted_iota(jnp.int32, (tm, tm), 0)
    col = lax.broadcasted_iota(jnp.int32, (tm, tm), 1)
    lower = jnp.where(col < row, 1.0, 0.0).astype(BF16)
    before = _dot(lower, picked.astype(BF16)) + carry_ref[0:1, :]
    ri = jnp.zeros((tm, LANES), jnp.int32)
    rg = jnp.zeros((tm, LANES), F32)
    for k in range(TOP_K):
        rank = jnp.sum(jnp.where(hots[k], before, 0.0), axis=-1, keepdims=True)
        ri = jnp.where(lane == k, tops[k][1], ri)
        ri = jnp.where(lane == TOP_K + k, rank.astype(jnp.int32), ri)
        rg = jnp.where(lane == k, es[k] / esum, rg)
    ri_ref[...] = ri
    rg_ref[...] = rg
    total = carry_ref[...] + jnp.sum(picked, axis=0, keepdims=True)
    carry_ref[...] = total
    cnt_ref[...] = total


def _merge(x2, ofox, ods, omem, wg, wbf, wbd, wbm, wout, g1, b1, wr, br):
    s = x2.shape[0]
    tm = min(MERGE_TM, s)
    full = lambda a: pl.BlockSpec(a.shape, lambda i: (0,) * a.ndim)
    rows = lambda w: pl.BlockSpec((tm, w), lambda i: (i, 0))
    od_specs = [pl.BlockSpec((dil, tm // dil, 2 * LANES), lambda i: (0, i, 0)) for _, dil in DIL_CONFIGS]
    return pl.pallas_call(
        _merge_kernel,
        grid=(s // tm,),
        in_specs=[rows(D_MODEL), rows(FOX_W)] + od_specs +
                 [rows(MEM_W), full(wg), full(wbf), full(wbd), full(wbm), full(wout),
                  full(g1), full(b1), full(wr), full(br)],
        out_specs=[rows(D_MODEL), pl.BlockSpec((tm * SUBLANES, LANES), lambda i: (i, 0)),
                   rows(LANES), rows(LANES), pl.BlockSpec((8, LANES), lambda i: (0, 0))],
        out_shape=[jax.ShapeDtypeStruct((s, D_MODEL), F32),
                   jax.ShapeDtypeStruct((s * SUBLANES, LANES), F32),
                   jax.ShapeDtypeStruct((s, LANES), jnp.int32),
                   jax.ShapeDtypeStruct((s, LANES), F32),
                   jax.ShapeDtypeStruct((8, LANES), F32)],
        scratch_shapes=[pltpu.VMEM((8, LANES), F32), pltpu.VMEM((DIL_GROUPS, 2, tm, LANES), F32)],
        compiler_params=pltpu.CompilerParams(dimension_semantics=("arbitrary",),
                                             vmem_limit_bytes=VMEM_LIMIT),
        name="merge",
    )(x2, ofox, *ods, omem, wg, wbf, wbd, wbm, wout, g1, b1, wr, br)


def _row_copy(src_ref, src_off, dst_ref, dst_off, sem):
    tile = lambda ref, off: ref.at[pl.ds(pl.multiple_of(off, SUBLANES), SUBLANES), :]
    return pltpu.make_async_copy(tile(src_ref, src_off), tile(dst_ref, dst_off), sem)


def _dispatch_kernel(dst_ref, lo_ref, hi_ref, nu_ref, h_ref, xb_ref, zero_ref, hbuf_ref, sem, zsem, lsem, *,
                     s_tiles):
    i = pl.program_id(0)
    n = pl.num_programs(0)
    rows = hbuf_ref.shape[0] // DISPATCH_SLOTS
    tm = rows // SUBLANES
    tile_rows = zero_ref.shape[0]
    n_tiles = xb_ref.shape[0] // tile_rows

    def tail_tile(b):
        return pltpu.make_async_copy(
            zero_ref, xb_ref.at[pl.ds(pl.multiple_of(b * tile_rows, tile_rows), tile_rows), :], zsem)

    def zero_fill(wait):
        def pad_rows(e, _):
            lo = lo_ref[e]
            length = hi_ref[e] - lo
            size = tile_rows // SUBLANES // 2
            while size >= 1:
                @pl.when(jnp.bitwise_and(length, size) != 0)
                def _(size=size):
                    pos = lo + jnp.bitwise_and(length, -2 * size)
                    cp = pltpu.make_async_copy(
                        zero_ref.at[pl.ds(0, size * SUBLANES), :],
                        xb_ref.at[pl.ds(pl.multiple_of(pos * SUBLANES, SUBLANES), size * SUBLANES), :], zsem)
                    cp.wait() if wait else cp.start()
                size //= 2
            return 0
        lax.fori_loop(0, N_EXPERTS, pad_rows, 0)

        def tail(b, _):
            tail_tile(b).wait() if wait else tail_tile(b).start()
            return 0
        lax.fori_loop(nu_ref[0], n_tiles, tail, 0)

    def load(tile, slot):
        return pltpu.make_async_copy(h_ref.at[pl.ds(pl.multiple_of(tile * rows, rows), rows), :],
                                     hbuf_ref.at[pl.ds(pl.multiple_of(slot * rows, rows), rows), :],
                                     lsem.at[slot])

    def wait_scatter(slot):
        def wait(t, _):
            for k in range(TOP_K):
                _row_copy(hbuf_ref, 0, xb_ref, 0, sem.at[slot]).wait()
            return 0
        lax.fori_loop(0, tm, wait, 0, unroll=ROW_DMA_UNROLL)

    slot = i % DISPATCH_SLOTS
    nxt = (i + 1) % DISPATCH_SLOTS

    @pl.when(i == 0)
    def _():
        load(0, 0).start()
        zero_ref[...] = jnp.zeros(zero_ref.shape, F32)
        zero_fill(False)
        zero_fill(True)

    load(i, slot).wait()

    @pl.when(i >= DISPATCH_SLOTS - 1)
    def _():
        wait_scatter(nxt)

    @pl.when(i + 1 < n)
    def _():
        load(i + 1, nxt).start()

    def start(t, _):
        for k in range(TOP_K):
            _row_copy(hbuf_ref, slot * rows + t * SUBLANES, xb_ref, dst_ref[t * TOP_K + k],
                      sem.at[slot]).start(priority=k % 2)
        return 0

    lax.fori_loop(0, tm, start, 0, unroll=ROW_DMA_UNROLL)

    @pl.when(i == n - 1)
    def _():
        for back in range(min(DISPATCH_SLOTS - 1, s_tiles)):
            wait_scatter((i - back) % DISPATCH_SLOTS)


def _dispatch(dst8, pad_lo, pad_hi, n_used, h3, n_tiles, tile_tokens):
    s = h3.shape[0] // SUBLANES
    tm = min(DISPATCH_TM, s)
    smem = pl.BlockSpec(memory_space=pltpu.SMEM)
    body = functools.partial(_dispatch_kernel, s_tiles=s // tm)
    return pl.pallas_call(
        body,
        grid=(s // tm,),
        in_specs=[pl.BlockSpec((tm * TOP_K,), lambda i: (i,), memory_space=pltpu.SMEM), smem, smem, smem,
                  pl.BlockSpec(memory_space=pl.ANY)],
        out_specs=pl.BlockSpec(memory_space=pl.ANY),
        out_shape=jax.ShapeDtypeStruct((n_tiles * tile_tokens * SUBLANES, LANES), F32),
        scratch_shapes=[pltpu.VMEM((tile_tokens * SUBLANES, LANES), F32),
                        pltpu.VMEM((DISPATCH_SLOTS * tm * SUBLANES, LANES), F32),
                        pltpu.SemaphoreType.DMA((DISPATCH_SLOTS,)), pltpu.SemaphoreType.DMA(()),
                        pltpu.SemaphoreType.DMA((DISPATCH_SLOTS,))],
        compiler_params=pltpu.CompilerParams(dimension_semantics=("arbitrary",),
                                             vmem_limit_bytes=VMEM_LIMIT),
        name="dispatch",
    )(dst8, pad_lo, pad_hi, n_used, h3)


def _expert_kernel(te_ref, nu_ref, x_ref, w1_ref, b1_ref, w2_ref, b2_ref, y_ref, w1b_ref, w2b_ref):
    b = pl.program_id(0)

    @pl.when(b >= nu_ref[0])
    def _():
        y_ref[...] = jnp.zeros(y_ref.shape, y_ref.dtype)

    @pl.when(b < nu_ref[0])
    def _():
        prev = te_ref[jnp.maximum(b - 1, 0)]

        @pl.when((b == 0) | (te_ref[b] != prev))
        def _():
            chunk = 128
            def cast(r, _):
                rows = pl.ds(pl.multiple_of(r * chunk, chunk), chunk)
                w1b_ref[rows, :] = w1_ref[rows, :].astype(BF16)
                w2b_ref[rows, :] = w2_ref[rows, :].astype(BF16)
                return 0
            lax.fori_loop(0, D_MODEL // chunk, cast, 0)

        xe = _load_token_tiles(x_ref, 0, x_ref.shape[0] // SUBLANES).astype(BF16)
        hcat = _dot(xe, w1b_ref[...]) + b1_ref[...]
        h_glu = jnp.minimum(hcat[:, :D_FF], SWIGLU_LIMIT)
        h_lin = jnp.clip(hcat[:, D_FF:], -SWIGLU_LIMIT, SWIGLU_LIMIT)
        act = h_glu * jax.nn.sigmoid(SWIGLU_ALPHA * h_glu) * (h_lin + 1.0)
        _store_token_tiles(y_ref, _dot(act.astype(BF16), w2b_ref[...]) + b2_ref[...])


def _experts(tile_e, n_used, xb, w1, b1, w2, b2):
    p = xb.shape[0] // SUBLANES
    tm = EXPERT_TM
    row_map = lambda b, te, nu: (jnp.minimum(b, nu[0] - 1), 0)
    exp_map = lambda b, te, nu: (te[b], 0, 0)
    return pl.pallas_call(
        _expert_kernel,
        grid_spec=pltpu.PrefetchScalarGridSpec(
            num_scalar_prefetch=2,
            grid=(p // tm,),
            in_specs=[pl.BlockSpec((tm * SUBLANES, LANES), row_map),
                      pl.BlockSpec((None, D_MODEL, 2 * D_FF), exp_map),
                      pl.BlockSpec((None, 1, 2 * D_FF), exp_map),
                      pl.BlockSpec((None, D_FF, D_MODEL), exp_map),
                      pl.BlockSpec((None, 1, D_MODEL), exp_map)],
            out_specs=pl.BlockSpec((tm * SUBLANES, LANES), lambda b, te, nu: (b, 0)),
            scratch_shapes=[pltpu.VMEM((D_MODEL, 2 * D_FF), BF16), pltpu.VMEM((D_FF, D_MODEL), BF16)]),
        out_shape=jax.ShapeDtypeStruct((p * SUBLANES, LANES), F32),
        compiler_params=pltpu.CompilerParams(dimension_semantics=("arbitrary",),
                                             vmem_limit_bytes=VMEM_LIMIT),
        name="experts",
    )(tile_e, n_used, xb, w1, b1, w2, b2)


def _combine_kernel(cur_ref, nxt_ref, h_ref, rg_ref, g_ref, b_ref, yb_ref, o_ref, buf_ref, sem):
    i = pl.program_id(0)
    tm = h_ref.shape[0]
    slot_rows = TOP_K * tm * SUBLANES
    slot = i % 2

    def gather(src_ref, into):
        def start(t, _):
            for k in range(TOP_K):
                _row_copy(yb_ref, src_ref[t * TOP_K + k], buf_ref, into * slot_rows + (k * tm + t) * SUBLANES,
                          sem.at[into]).start(priority=k % 2)
            return 0
        lax.fori_loop(0, tm, start, 0, unroll=ROW_DMA_UNROLL)

    @pl.when(i == 0)
    def _():
        gather(cur_ref, 0)

    def wait(t, _):
        for k in range(TOP_K):
            _row_copy(yb_ref, 0, buf_ref, 0, sem.at[slot]).wait()
        return 0

    lax.fori_loop(0, tm, wait, 0, unroll=ROW_DMA_UNROLL)

    @pl.when(i + 1 < pl.num_programs(0))
    def _():
        gather(nxt_ref, 1 - slot)

    h = h_ref[...]
    rg = rg_ref[...]
    moe = None
    for k in range(TOP_K):
        t = rg[:, k:k + 1] * _load_token_tiles(buf_ref, slot * (TOP_K * tm) + k * tm, tm)
        moe = t if moe is None else moe + t
    o_ref[...] = _layer_norm(DEEPNORM_ALPHA * h + moe, g_ref[...], b_ref[...])


def _combine(dst8, h, rg, g2, b2, yb):
    s = h.shape[0]
    tm = min(COMBINE_TM, s)
    last = s // tm - 1
    full = lambda a: pl.BlockSpec(a.shape, lambda i: (0,) * a.ndim)
    return pl.pallas_call(
        _combine_kernel,
        grid=(s // tm,),
        in_specs=[pl.BlockSpec((tm * TOP_K,), lambda i: (i,), memory_space=pltpu.SMEM),
                  pl.BlockSpec((tm * TOP_K,), lambda i: (jnp.minimum(i + 1, last),), memory_space=pltpu.SMEM),
                  pl.BlockSpec((tm, D_MODEL), lambda i: (i, 0)),
                  pl.BlockSpec((tm, LANES), lambda i: (i, 0)),
                  full(g2), full(b2),
                  pl.BlockSpec(memory_space=pl.ANY)],
        out_specs=pl.BlockSpec((tm, D_MODEL), lambda i: (i, 0)),
        out_shape=jax.ShapeDtypeStruct((s, D_MODEL), F32),
        scratch_shapes=[pltpu.VMEM((2 * TOP_K * tm * SUBLANES, LANES), F32), pltpu.SemaphoreType.DMA((2,))],
        compiler_params=pltpu.CompilerParams(dimension_semantics=("arbitrary",),
                                             vmem_limit_bytes=VMEM_LIMIT),
        name="combine",
    )(dst8, dst8, h, rg, g2, b2, yb)


def kernel(x, mem, w_in, b_fgate, t5_bias, w_mem_kv, w_br_fox, w_br_dil, w_br_mem, w_out, ln1_g, ln1_b,
           w_router, b_router, w_exp_in, b_exp_in, w_exp_out, b_exp_out, ln2_g, ln2_b):
    bsz, s, d = x.shape
    assert bsz == 1 and d == D_MODEL and w_in.shape[0] == 1
    assert s % (DIL_CONFIGS[-1][1] * DBLK) == 0
    x2 = x[0]
    w_in0 = w_in[0]

    fg_lanes = np.array([FG_PAIR_STRIDE * (h // 2) + FG_PIECES * (h % 2) + j
                         for h in range(FOX_HEADS) for j in range(FG_PIECES)])
    fg_heads = np.repeat(np.arange(FOX_HEADS), FG_PIECES)
    w_fg = jnp.zeros((D_MODEL, LANES), F32).at[:, fg_lanes].set(w_in0[:, OFF_FOX_F + fg_heads])
    bf = jnp.zeros((1, LANES), F32).at[0, fg_lanes].set(b_fgate[0][fg_heads])
    w_all = jnp.concatenate([w_in0[:, OFF_FOX_QKV:OFF_FOX_QKV + 2 * FOX_W], w_in0[:, OFF_DIL_QKV:OFF_GATES],
                             w_fg], axis=1).astype(BF16)
    wvt = w_in0[:, OFF_FOX_QKV + 2 * FOX_W:OFF_FOX_F].T.astype(BF16)
    w_gates = w_in0[:, OFF_GATES:].astype(BF16)

    qk, ext, vt, o_mem, *dil_qkv, stats = _proj(x2, w_all, wvt, bf, mem[0].astype(BF16),
                                                w_mem_kv[0].astype(BF16))

    tq = min(FOX_TQ, s)
    assert min(PROJ_TM, s) == min(FOX_TK, tq)
    nblk = stats.shape[0]
    gate_lane = np.array([FG_PAIR_STRIDE * (h // 2) + FG_PIECES * (h % 2) for h in range(FOX_HEADS)])
    q_norm = jnp.sqrt(stats[:, 0, :FOX_HEADS, 0])
    k_norm = lax.cummax(jnp.sqrt(stats[:, 1, :FOX_HEADS, 0]), axis=0)
    neg_c = lax.cummax(stats[:, 2, 0, gate_lane], axis=0)
    q_norm = jnp.max(q_norm.reshape(s // tq, nblk * tq // s, FOX_HEADS), axis=1)
    o_fox = _fox(k_norm.T.reshape(-1), neg_c.T.reshape(-1), q_norm.T.reshape(-1), qk, ext, vt)

    ii = np.arange(DBLK, dtype=np.int32)[:, None]
    jj = np.arange(2 * DBLK, dtype=np.int32)[None, :]
    rel = np.clip(DBLK + ii - jj, 0, None)
    table_flat = t5_bias.reshape(-1)
    ods = []
    for g, (window, dil) in enumerate(DIL_CONFIGS):
        assert window // dil == DBLK
        bucket = _t5_bucket(jnp.asarray(rel * dil, dtype=jnp.int32))
        od = _dil(g, dil, table_flat, dil_qkv[g].reshape(s, DIL_QKV_W), bucket)
        ods.append(od.reshape(dil, s // dil, 2 * LANES))

    wr = jnp.zeros((D_MODEL, LANES), F32).at[:, :N_EXPERTS].set(w_router[0]).astype(BF16)
    br = jnp.zeros((1, LANES), F32).at[0, :N_EXPERTS].set(b_router[0])
    h1, h3, ri, rg, cnt = _merge(x2, o_fox, ods, o_mem, w_gates, w_br_fox[0].astype(BF16),
                                 w_br_dil[0].astype(BF16), w_br_mem[0].astype(BF16), w_out[0].astype(BF16),
                                 ln1_g, ln1_b, wr, br)

    tm = EXPERT_TM
    counts = cnt[0, :N_EXPERTS].astype(jnp.int32)
    padded = (counts + tm - 1) // tm * tm
    pad_ends = jnp.cumsum(padded)
    pad_starts = (pad_ends - padded).astype(jnp.int32)
    n_tiles = (s * TOP_K) // tm + N_EXPERTS
    n_used = (pad_ends[-1] // tm).astype(jnp.int32)
    tile_lo = jnp.arange(n_tiles, dtype=jnp.int32) * tm
    tile_e = jnp.minimum(jnp.sum((pad_ends[None, :] <= tile_lo[:, None]).astype(jnp.int32), axis=1),
                         N_EXPERTS - 1)
    tile_e = tile_e[jnp.minimum(jnp.arange(n_tiles), n_used - 1)]
    e_tk = ri[:, 0:TOP_K]
    start_tk = jnp.sum(jnp.where(e_tk[:, :, None] == jnp.arange(N_EXPERTS, dtype=jnp.int32),
                                 pad_starts, 0), axis=-1)
    dst8 = ((start_tk + ri[:, TOP_K:2 * TOP_K]) * SUBLANES).reshape(-1)

    xb = _dispatch(dst8, pad_starts + counts, pad_ends.astype(jnp.int32), n_used.reshape(1), h3, n_tiles, tm)
    yb = _experts(tile_e, n_used.reshape(1), xb, w_exp_in[0], b_exp_in[0][:, None, :], w_exp_out[0],
                  b_exp_out[0][:, None, :])
    out = _combine(dst8, h1, rg, ln2_g, ln2_b, yb)
    return out[None]
```

```python
import functools

import numpy as np
import jax
import jax.numpy as jnp
from jax import lax
from jax.experimental import pallas as pl
from jax.experimental.pallas import tpu as pltpu

D_MODEL = 1024
HEAD_DIM = 64
LANES = 128
SUBLANES = 8
assert D_MODEL == SUBLANES * LANES
FOX_HEADS = 6
FOX_PAIRS = FOX_HEADS // 2
DIL_CONFIGS = ((128, 1), (512, 4), (2048, 16))
DIL_GROUPS = len(DIL_CONFIGS)
DIL_HEADS = 2 * DIL_GROUPS
MEM_HEADS = 4
MEM_PAIRS = MEM_HEADS // 2
MEM_LEN = 256
FOX_W = FOX_HEADS * HEAD_DIM
DIL_W = DIL_HEADS * HEAD_DIM
MEM_W = MEM_HEADS * HEAD_DIM
DBLK = 128
DIL_QKV_W = 3 * LANES
T5_BUCKETS = 32
T5_MAX_EXACT = T5_BUCKETS // 2
T5_MAX_DISTANCE = 2048
N_EXPERTS = 32
TOP_K = 4
D_FF = D_MODEL
SWIGLU_LIMIT = 7.0
SWIGLU_ALPHA = 1.702
LN_EPS = 1e-5
DEEPNORM_ALPHA = 2.0 ** 0.25
QK_SCALE = HEAD_DIM ** -0.5
F32_EXP_ZERO_BELOW = 110.0

OFF_FOX_QKV = 0
OFF_FOX_F = OFF_FOX_QKV + 3 * FOX_W
OFF_DIL_QKV = OFF_FOX_F + FOX_HEADS
OFF_MEM_Q = OFF_DIL_QKV + 3 * DIL_W
OFF_GATES = OFF_MEM_Q + MEM_W

W_FOXQ = 0
W_FOXK = FOX_W
W_DIL = 2 * FOX_W
W_MEMQ = W_DIL + 3 * DIL_W
W_FGATE = W_MEMQ + MEM_W
W_ALL = W_FGATE + LANES
FG_PAIR_STRIDE = 8
FG_PIECES = 3
VT_ROWS = HEAD_DIM + 16

VMEM_LIMIT = 56 * 1024 * 1024

PROJ_TM = 512
FOX_TQ = 1024
FOX_TK = 512
FOX_TC = 1024
DIL_CHUNK = 512
MERGE_TM = 512
DISPATCH_TM = 256
DISPATCH_SLOTS = 3
EXPERT_TM = 512
COMBINE_TM = 256
ROW_DMA_UNROLL = 4

BF16 = jnp.bfloat16
F32 = jnp.float32


def _dot(a, b):
    return jnp.dot(a, b, preferred_element_type=F32)


def _dot_nt(a, b):
    return lax.dot_general(a, b, (((1,), (1,)), ((), ())), preferred_element_type=F32)


def _lane_lt(shape, n):
    return lax.broadcasted_iota(jnp.int32, shape, len(shape) - 1) < n


def _split3(v):
    hi = v.astype(BF16)
    r1 = v - hi.astype(F32)
    mid = r1.astype(BF16)
    lo = (r1 - mid.astype(F32)).astype(BF16)
    return hi, mid, lo


def _store_token_tiles(ref, v):
    n = v.shape[0]
    for c in range(SUBLANES):
        ref[pl.ds(c, n, stride=SUBLANES), :] = v[:, c * LANES:(c + 1) * LANES]


def _load_token_tiles(ref, lo, n):
    return jnp.concatenate([ref[pl.ds(lo * SUBLANES + c, n, stride=SUBLANES), :] for c in range(SUBLANES)],
                           axis=1)


def _layer_norm(r, g, b):
    mu = jnp.mean(r, axis=-1, keepdims=True)
    d = r - mu
    var = jnp.mean(d * d, axis=-1, keepdims=True)
    return d * lax.rsqrt(var + LN_EPS) * g + b


def _proj_kernel(x_ref, w_ref, wvt_ref, bf_ref, mem_ref, wkv_ref, qk_ref, ext_ref, vt_ref, omem_ref,
                 d0_ref, d1_ref, d2_ref, stat_ref, kvm_ref, carry_ref, stage_ref):
    i = pl.program_id(0)
    tm = x_ref.shape[0]

    @pl.when(i == 0)
    def _():
        kvm_ref[...] = _dot(mem_ref[...], wkv_ref[...]).astype(BF16)
        carry_ref[...] = jnp.zeros_like(carry_ref)

    xb = x_ref[...].astype(BF16)
    proj = _dot(xb, w_ref[...])

    qb = (proj[:, W_FOXQ:W_FOXK] * QK_SCALE).astype(BF16)
    kb = proj[:, W_FOXK:W_DIL].astype(BF16)
    qk_ref[:, W_FOXQ:W_FOXK] = qb
    qk_ref[:, W_FOXK:W_DIL] = kb

    stat_ref[...] = jnp.zeros(stat_ref.shape, F32)
    first_head = _lane_lt((tm, LANES), HEAD_DIM)
    for which, vb in enumerate((qb, kb)):
        for p in range(FOX_PAIRS):
            sq = jnp.square(vb[:, p * LANES:(p + 1) * LANES].astype(F32))
            for a in range(2):
                mask = first_head if a == 0 else jnp.logical_not(first_head)
                norm2 = jnp.sum(jnp.where(mask, sq, 0.0), axis=1, keepdims=True)
                top = jnp.max(norm2, axis=0, keepdims=True)
                stat_ref[0, which, 2 * p + a:2 * p + a + 1, :] = jnp.broadcast_to(top, (1, LANES))

    for g, (d_ref, (_, dil)) in enumerate(zip((d0_ref, d1_ref, d2_ref), DIL_CONFIGS)):
        cols = [proj[:, W_DIL + part * DIL_W + g * LANES:W_DIL + part * DIL_W + (g + 1) * LANES]
                for part in range(3)]
        cols[0] = cols[0] * QK_SCALE
        if dil == 1:
            for part in range(3):
                d_ref[0, :, part * LANES:(part + 1) * LANES] = cols[part].astype(BF16)
        else:
            for part in range(3):
                stage_ref[part] = cols[part]
            for r in range(dil):
                for part in range(3):
                    d_ref[r, :, part * LANES:(part + 1) * LANES] = (
                        stage_ref[part, pl.ds(r, tm // dil, stride=dil), :].astype(BF16))

    vt = _dot_nt(wvt_ref[...], xb)
    for h in range(FOX_HEADS):
        vt_ref[h, 0:HEAD_DIM, :] = vt[h * HEAD_DIM:(h + 1) * HEAD_DIM, :].astype(BF16)
        vt_ref[h, HEAD_DIM:VT_ROWS, :] = jnp.ones((VT_ROWS - HEAD_DIM, tm), BF16)

    z = proj[:, W_FGATE:W_ALL] + bf_ref[...]
    logf = jnp.minimum(z, 0.0) - jnp.log1p(jnp.exp(-jnp.abs(z)))
    row = lax.broadcasted_iota(jnp.int32, (tm, tm), 0)
    col = lax.broadcasted_iota(jnp.int32, (tm, tm), 1)
    lower = jnp.where(col <= row, 1.0, 0.0).astype(BF16)
    hi, mid, lo = _split3(logf)
    c = (_dot(lower, lo) + _dot(lower, mid)) + _dot(lower, hi) + carry_ref[0:1, :]
    carry_ref[...] = jnp.broadcast_to(c[tm - 1:tm, :], carry_ref.shape)
    stat_ref[0, 2, 0:1, :] = jnp.max(-c, axis=0, keepdims=True)
    nhi, nmid, nlo = _split3(-c)
    lane = lax.broadcasted_iota(jnp.int32, (tm, LANES), 1)
    slot = lane & (FG_PAIR_STRIDE - 1)
    piece = jnp.where(slot >= FG_PIECES, slot - FG_PIECES, slot)
    used = (slot < 2 * FG_PIECES) & (lane < FOX_PAIRS * FG_PAIR_STRIDE)
    ext = jnp.where(piece == 0, nhi, jnp.where(piece == 1, nmid, nlo))
    ext_ref[...] = jnp.where(used, ext, jnp.zeros_like(ext))

    qm = (proj[:, W_MEMQ:W_FGATE] * QK_SCALE).astype(BF16)
    for p in range(MEM_PAIRS):
        qp = qm[:, p * LANES:(p + 1) * LANES]
        kp = kvm_ref[:, p * LANES:(p + 1) * LANES]
        vp = kvm_ref[:, MEM_W + p * LANES:MEM_W + (p + 1) * LANES]
        first = _lane_lt(qp.shape, HEAD_DIM)
        outs = []
        for a in range(2):
            qa = jnp.where(first if a == 0 else jnp.logical_not(first), qp, jnp.zeros_like(qp))
            s = _dot_nt(qa, kp)
            m = jnp.max(s, axis=-1, keepdims=True)
            e = jnp.exp(s - m)
            den = jnp.sum(e, axis=-1, keepdims=True)
            outs.append(_dot(e.astype(BF16), vp) / den)
        omem_ref[:, p * LANES:(p + 1) * LANES] = jnp.where(first, outs[0], outs[1]).astype(BF16)


def _proj(x2, w_all, wvt, bf, mem_b, wkv_b):
    s = x2.shape[0]
    tm = min(PROJ_TM, s)
    full = lambda shape: pl.BlockSpec(shape, lambda i: (0,) * len(shape))
    rows = lambda w: pl.BlockSpec((tm, w), lambda i: (i, 0))
    dil_spec = lambda dil: pl.BlockSpec((dil, tm // dil, DIL_QKV_W), lambda i: (0, i, 0))
    dil_shape = lambda dil: jax.ShapeDtypeStruct((dil, s // dil, DIL_QKV_W), BF16)
    dils = [dil for _, dil in DIL_CONFIGS]
    return pl.pallas_call(
        _proj_kernel,
        grid=(s // tm,),
        in_specs=[rows(D_MODEL),
                  full(w_all.shape), full(wvt.shape), full(bf.shape), full(mem_b.shape), full(wkv_b.shape)],
        out_specs=[rows(W_DIL), rows(LANES),
                   pl.BlockSpec((FOX_HEADS, VT_ROWS, tm), lambda i: (0, 0, i)),
                   rows(MEM_W)] + [dil_spec(d) for d in dils] +
                  [pl.BlockSpec((1, 3, SUBLANES, LANES), lambda i: (i, 0, 0, 0))],
        out_shape=[jax.ShapeDtypeStruct((s, W_DIL), BF16),
                   jax.ShapeDtypeStruct((s, LANES), BF16),
                   jax.ShapeDtypeStruct((FOX_HEADS, VT_ROWS, s), BF16),
                   jax.ShapeDtypeStruct((s, MEM_W), BF16)] + [dil_shape(d) for d in dils] +
                  [jax.ShapeDtypeStruct((s // tm, 3, SUBLANES, LANES), F32)],
        scratch_shapes=[pltpu.VMEM((MEM_LEN, 2 * MEM_W), BF16), pltpu.VMEM((8, LANES), F32),
                        pltpu.VMEM((3, tm, LANES), F32)],
        compiler_params=pltpu.CompilerParams(dimension_semantics=("arbitrary",),
                                             vmem_limit_bytes=VMEM_LIMIT),
        name="proj",
    )(x2, w_all, wvt, bf, mem_b, wkv_b)


def _fox_kernel(km_ref, nc_ref, qn_ref, q_ref, k_ref, e_ref, vt_ref, o_ref, m_ref, acc_ref, qx_ref, s_ref):
    p = pl.program_id(0)
    i = pl.program_id(1)
    t = q_ref.shape[0]
    tk = min(FOX_TK, t)
    q = q_ref[...]
    lane = lax.broadcasted_iota(jnp.int32, (t, LANES), 1)
    for a in range(2):
        head = (lane < HEAD_DIM) if a == 0 else (lane >= HEAD_DIM)
        lo = FG_PAIR_STRIDE * p + FG_PIECES * a
        ones = jnp.where((lane >= lo) & (lane < lo + FG_PIECES), 1.0, 0.0).astype(BF16)
        qx_ref[a, :, 0:LANES] = jnp.where(head, q, jnp.zeros_like(q))
        qx_ref[a, :, LANES:2 * LANES] = ones

    m_ref[...] = jnp.full(m_ref.shape, -jnp.inf, F32)
    acc_ref[...] = jnp.zeros(acc_ref.shape, F32)

    tc = min(FOX_TC, t)

    def phase(nxt, cur):
        if nxt is not None:
            nblk, nq0, nbuf = nxt
            nstart = pl.multiple_of(nblk * tk, tk)
            kx = jnp.concatenate([k_ref[pl.ds(nstart, tk), :], e_ref[pl.ds(nstart, tk), :]], axis=1)
        if cur is not None:
            cblk, cq0, cbuf, diagonal = cur
            cstart = pl.multiple_of(cblk * tk, tk)
            vtb = [vt_ref[a, :, pl.ds(cstart, tk)] for a in range(2)]
        for c in range(0, t, tc):
            if nxt is not None and c + tc > nq0:
                cols = slice(max(c, nq0), c + tc)
                for a in range(2):
                    s_ref[nbuf, a, :, cols] = _dot_nt(kx, qx_ref[a, cols, :])
            if cur is not None and c + tc > cq0:
                cols = slice(max(c, cq0), c + tc)
                for a in range(2):
                    s = s_ref[cbuf, a, :, cols]
                    if diagonal:
                        kpos = lax.broadcasted_iota(jnp.int32, s.shape, 0) + cq0
                        qpos = lax.broadcasted_iota(jnp.int32, s.shape, 1) + cols.start
                        s = jnp.where(kpos <= qpos, s, -jnp.inf)
                    m_prev = m_ref[a, :, cols]
                    m_new = jnp.maximum(m_prev, jnp.max(s, axis=0, keepdims=True))
                    alpha = jnp.exp(m_prev - m_new)
                    pt = jnp.exp(s - m_new).astype(BF16)
                    m_ref[a, :, cols] = m_new
                    acc_ref[a, :, cols] = alpha * acc_ref[a, :, cols] + _dot(vtb[a], pt)

    nd = t // tk
    nfull = i * nd
    phase((nfull, 0, 0), None)
    for d in range(nd):
        nxt = (nfull + d + 1, (d + 1) * tk, (d + 1) % 2) if d + 1 < nd else None
        phase(nxt, (nfull + d, d * tk, d % 2, True))

    nblocks = k_ref.shape[0] // tk
    nq = pl.num_programs(1)
    need = jnp.int32(0)
    for a in range(2):
        h = 2 * p + a
        thr = jnp.min(m_ref[a]) - F32_EXP_ZERO_BELOW
        qn = qn_ref[h * nq + i]
        count = lax.fori_loop(
            0, nfull,
            lambda j, n: n + (qn * km_ref[h * nblocks + j] + nc_ref[h * nblocks + j] >= thr).astype(jnp.int32),
            jnp.int32(0))
        need = jnp.maximum(need, count)
    need = jnp.minimum(jnp.bitwise_and(need + 1, -2), nfull)
    first = nfull - need

    @pl.when(need > 0)
    def _():
        phase((first, 0, 0), None)

        def pair(j, _):
            b0 = first + 2 * j
            phase((b0 + 1, 0, 1), (b0, 0, 0, False))
            phase((b0 + 2, 0, 0), (b0 + 1, 0, 1, False))
            return 0

        lax.fori_loop(0, need // 2 - 1, pair, 0)
        phase((nfull - 1, 0, 1), (nfull - 2, 0, 0, False))
        phase(None, (nfull - 1, 0, 1, False))
    o0 = acc_ref[0, 0:HEAD_DIM, :] / acc_ref[0, HEAD_DIM:HEAD_DIM + 1, :]
    o1 = acc_ref[1, 0:HEAD_DIM, :] / acc_ref[1, HEAD_DIM:HEAD_DIM + 1, :]
    o_ref[...] = jnp.concatenate([o0, o1], axis=0).T.astype(BF16)


def _fox(km, nc, qn, qk, ext, vt):
    s = qk.shape[0]
    t = min(FOX_TQ, s)
    tk = min(FOX_TK, t)
    assert s % t == 0 and t % (2 * tk) == 0
    kcol = W_FOXK // LANES
    return pl.pallas_call(
        _fox_kernel,
        grid=(FOX_PAIRS, s // t),
        in_specs=[pl.BlockSpec(memory_space=pltpu.SMEM), pl.BlockSpec(memory_space=pltpu.SMEM),
                  pl.BlockSpec(memory_space=pltpu.SMEM),
                  pl.BlockSpec((t, LANES), lambda p, i: (i, p)),
                  pl.BlockSpec((s, LANES), lambda p, i: (0, kcol + p)),
                  pl.BlockSpec((s, LANES), lambda p, i: (0, 0)),
                  pl.BlockSpec((2, VT_ROWS, s), lambda p, i: (p, 0, 0))],
        out_specs=pl.BlockSpec((t, LANES), lambda p, i: (i, p)),
        out_shape=jax.ShapeDtypeStruct((s, FOX_W), BF16),
        scratch_shapes=[pltpu.VMEM((2, 1, t), F32), pltpu.VMEM((2, VT_ROWS, t), F32),
                        pltpu.VMEM((2, t, 2 * LANES), BF16), pltpu.VMEM((2, 2, tk, t), F32)],
        compiler_params=pltpu.CompilerParams(dimension_semantics=("arbitrary", "arbitrary"),
                                             vmem_limit_bytes=VMEM_LIMIT),
        name="fox",
    )(km, nc, qn, qk, qk, ext, vt)


def _dil_kernel(table_ref, cur_ref, prev_ref, bucket_ref, o_ref, bias_ref, *, group, blocks_per_seq):
    j = pl.program_id(0)
    nblk = cur_ref.shape[0] // DBLK
    ii = lax.broadcasted_iota(jnp.int32, (DBLK, 2 * DBLK), 0)
    jj = lax.broadcasted_iota(jnp.int32, (DBLK, 2 * DBLK), 1)

    @pl.when(j == 0)
    def _():
        bucket = bucket_ref[...]
        rel = DBLK + ii - jj
        in_window = (rel >= 0) & (rel <= DBLK)
        for a in range(2):
            b = jnp.zeros(bucket.shape, F32)
            for t in range(T5_BUCKETS):
                b = jnp.where(bucket == t, table_ref[t * DIL_HEADS + 2 * group + a], b)
            bias_ref[a] = jnp.where(in_window, b, -jnp.inf)

    first = _lane_lt((DBLK, LANES), HEAD_DIM)
    chunk_has_prev = ((j * nblk) % blocks_per_seq) != 0
    for b in range(nblk):
        rows = slice(b * DBLK, (b + 1) * DBLK)
        q = cur_ref[rows, 0:LANES]
        if b == 0:
            kprev, vprev = prev_ref[:, LANES:2 * LANES], prev_ref[:, 2 * LANES:3 * LANES]
        else:
            before = slice((b - 1) * DBLK, b * DBLK)
            kprev, vprev = cur_ref[before, LANES:2 * LANES], cur_ref[before, 2 * LANES:3 * LANES]
        kcat = jnp.concatenate([kprev, cur_ref[rows, LANES:2 * LANES]], axis=0)
        vcat = jnp.concatenate([vprev, cur_ref[rows, 2 * LANES:3 * LANES]], axis=0)
        outs, lses = [], []
        for a in range(2):
            qa = jnp.where(first if a == 0 else jnp.logical_not(first), q, jnp.zeros_like(q))
            s = _dot_nt(qa, kcat) + bias_ref[a]
            if b == 0:
                s = jnp.where(chunk_has_prev | (jj >= DBLK), s, -jnp.inf)
            m = jnp.max(s, axis=-1, keepdims=True)
            e = jnp.exp(s - m)
            den = jnp.sum(e, axis=-1, keepdims=True)
            outs.append(_dot(e.astype(BF16), vcat) / den)
            lses.append(jnp.broadcast_to(m + jnp.log(den), (DBLK, LANES)))
        o_ref[rows, 0:LANES] = jnp.where(first, outs[0], outs[1])
        o_ref[rows, LANES:2 * LANES] = jnp.where(first, lses[0], lses[1])


def _dil(group, table_flat, qkv_sub, bucket):
    dil, seq, _ = qkv_sub.shape
    chunk = min(DIL_CHUNK, seq)
    assert seq % chunk == 0 and chunk % DBLK == 0
    nblk = chunk // DBLK
    cps = seq // chunk
    bps = seq // DBLK
    body = functools.partial(_dil_kernel, group=group, blocks_per_seq=bps)

    def prev_block(j):
        n = jnp.maximum(j * nblk - 1, 0)
        return (n // bps, n % bps, 0)

    return pl.pallas_call(
        body,
        grid=(dil * cps,),
        in_specs=[pl.BlockSpec(memory_space=pltpu.SMEM),
                  pl.BlockSpec((None, chunk, DIL_QKV_W), lambda j: (j // cps, j % cps, 0)),
                  pl.BlockSpec((None, DBLK, DIL_QKV_W), prev_block),
                  pl.BlockSpec((DBLK, 2 * DBLK), lambda j: (0, 0))],
        out_specs=pl.BlockSpec((None, chunk, 2 * LANES), lambda j: (j // cps, j % cps, 0)),
        out_shape=jax.ShapeDtypeStruct((dil, seq, 2 * LANES), F32),
        scratch_shapes=[pltpu.VMEM((2, DBLK, 2 * DBLK), F32)],
        compiler_params=pltpu.CompilerParams(dimension_semantics=("arbitrary",),
                                             vmem_limit_bytes=VMEM_LIMIT),
        name=f"dil{group}",
    )(table_flat, qkv_sub, qkv_sub, bucket)


def _t5_bucket(dist):
    is_small = dist < T5_MAX_EXACT
    nf = jnp.maximum(dist, T5_MAX_EXACT).astype(F32)
    large = T5_MAX_EXACT + (jnp.log(nf / T5_MAX_EXACT) / np.log(T5_MAX_DISTANCE / T5_MAX_EXACT)
                            * (T5_BUCKETS - T5_MAX_EXACT)).astype(jnp.int32)
    large = jnp.minimum(large, T5_BUCKETS - 1)
    return jnp.where(is_small, dist, large)


def _merge_kernel(x_ref, ofox_ref, od0_ref, od1_ref, od2_ref, omem_ref, wg_ref, wbf_ref, wbd_ref, wbm_ref,
                  wout_ref, g_ref, b_ref, wr_ref, br_ref, h_ref, h3_ref, ri_ref, rg_ref, cnt_ref,
                  carry_ref, tok_ref):
    i = pl.program_id(0)
    tm = x_ref.shape[0]

    @pl.when(i == 0)
    def _():
        carry_ref[...] = jnp.zeros_like(carry_ref)

    x = x_ref[...]
    xb = x.astype(BF16)


    for g, (od_ref, (_, dil)) in enumerate(zip((od0_ref, od1_ref, od2_ref), DIL_CONFIGS)):
        for r in range(dil):
            for half in range(2):
                tok_ref[g, half, pl.ds(r, tm // dil, stride=dil), :] = od_ref[r, :, half * LANES:(half + 1) * LANES]
    lse = [tok_ref[g, 1] for g in range(DIL_GROUPS)]
    mx = jnp.maximum(jnp.maximum(lse[0], lse[1]), lse[2])
    ex = [jnp.exp(l - mx) for l in lse]
    den = ex[0] + ex[1] + ex[2]
    o_dil = jnp.concatenate([(tok_ref[g, 0] * (ex[g] / den)).astype(BF16) for g in range(DIL_GROUPS)], axis=1)

    def gate(br):
        return jax.nn.sigmoid(_dot(xb, wg_ref[:, br * D_MODEL:(br + 1) * D_MODEL]))

    merged = gate(0) * _dot(ofox_ref[...], wbf_ref[...])
    merged = merged + gate(1) * _dot(o_dil, wbd_ref[...])
    merged = merged + gate(2) * _dot(omem_ref[...], wbm_ref[...])

    y = _dot(merged.astype(BF16), wout_ref[...])
    h = _layer_norm(DEEPNORM_ALPHA * x + y, g_ref[...], b_ref[...])
    h_ref[...] = h
    _store_token_tiles(h3_ref, h)

    lane = lax.broadcasted_iota(jnp.int32, (tm, LANES), 1)
    logits = _dot(h.astype(BF16), wr_ref[...]) + br_ref[...]
    vals = jnp.where(lane < N_EXPERTS, logits, -jnp.inf)
    tops, hots = [], []
    for _ in range(TOP_K):
        mk = jnp.max(vals, axis=-1, keepdims=True)
        ik = jnp.min(jnp.where(vals == mk, lane, LANES), axis=-1, keepdims=True)
        hot = lane == ik
        vals = jnp.where(hot, -jnp.inf, vals)
        tops.append((mk, ik))
        hots.append(hot)
    es = [jnp.exp(mk - tops[0][0]) for mk, _ in tops]
    esum = es[0] + es[1] + es[2] + es[3]

    picked = jnp.where(hots[0] | hots[1] | hots[2] | hots[3], 1.0, 0.0)
    row = lax.broadcasted_iota(jnp.int32, (tm, tm), 0)
    col = lax.broadcasted_iota(jnp.int32, (tm, tm), 1)
    lower = jnp.where(col < row, 1.0, 0.0).astype(BF16)
    before = _dot(lower, picked.astype(BF16)) + carry_ref[0:1, :]
    ri = jnp.zeros((tm, LANES), jnp.int32)
    rg = jnp.zeros((tm, LANES), F32)
    for k in range(TOP_K):
        rank = jnp.sum(jnp.where(hots[k], before, 0.0), axis=-1, keepdims=True)
        ri = jnp.where(lane == k, tops[k][1], ri)
        ri = jnp.where(lane == TOP_K + k, rank.astype(jnp.int32), ri)
        rg = jnp.where(lane == k, es[k] / esum, rg)
    ri_ref[...] = ri
    rg_ref[...] = rg
    total = carry_ref[...] + jnp.sum(picked, axis=0, keepdims=True)
    carry_ref[...] = total
    cnt_ref[...] = total


def _merge(x2, ofox, ods, omem, wg, wbf, wbd, wbm, wout, g1, b1, wr, br):
    s = x2.shape[0]
    tm = min(MERGE_TM, s)
    full = lambda a: pl.BlockSpec(a.shape, lambda i: (0,) * a.ndim)
    rows = lambda w: pl.BlockSpec((tm, w), lambda i: (i, 0))
    od_specs = [pl.BlockSpec((dil, tm // dil, 2 * LANES), lambda i: (0, i, 0)) for _, dil in DIL_CONFIGS]
    return pl.pallas_call(
        _merge_kernel,
        grid=(s // tm,),
        in_specs=[rows(D_MODEL), rows(FOX_W)] + od_specs +
                 [rows(MEM_W), full(wg), full(wbf), full(wbd), full(wbm), full(wout),
                  full(g1), full(b1), full(wr), full(br)],
        out_specs=[rows(D_MODEL), pl.BlockSpec((tm * SUBLANES, LANES), lambda i: (i, 0)),
                   rows(LANES), rows(LANES), pl.BlockSpec((8, LANES), lambda i: (0, 0))],
        out_shape=[jax.ShapeDtypeStruct((s, D_MODEL), F32),
                   jax.ShapeDtypeStruct((s * SUBLANES, LANES), F32),
                   jax.ShapeDtypeStruct((s, LANES), jnp.int32),
                   jax.ShapeDtypeStruct((s, LANES), F32),
                   jax.ShapeDtypeStruct((8, LANES), F32)],
        scratch_shapes=[pltpu.VMEM((8, LANES), F32), pltpu.VMEM((DIL_GROUPS, 2, tm, LANES), F32)],
        compiler_params=pltpu.CompilerParams(dimension_semantics=("arbitrary",),
                                             vmem_limit_bytes=VMEM_LIMIT),
        name="merge",
    )(x2, ofox, *ods, omem, wg, wbf, wbd, wbm, wout, g1, b1, wr, br)


def _row_copy(src_ref, src_off, dst_ref, dst_off, sem):
    tile = lambda ref, off: ref.at[pl.ds(pl.multiple_of(off, SUBLANES), SUBLANES), :]
    return pltpu.make_async_copy(tile(src_ref, src_off), tile(dst_ref, dst_off), sem)


def _dispatch_kernel(dst_ref, lo_ref, hi_ref, nu_ref, h_ref, xb_ref, zero_ref, hbuf_ref, sem, zsem, lsem, *,
                     s_tiles):
    i = pl.program_id(0)
    n = pl.num_programs(0)
    rows = hbuf_ref.shape[0] // DISPATCH_SLOTS
    tm = rows // SUBLANES
    tile_rows = zero_ref.shape[0]
    n_tiles = xb_ref.shape[0] // tile_rows

    def tail_tile(b):
        return pltpu.make_async_copy(
            zero_ref, xb_ref.at[pl.ds(pl.multiple_of(b * tile_rows, tile_rows), tile_rows), :], zsem)

    def zero_fill(wait):
        def pad_rows(e, _):
            lo = lo_ref[e]
            length = hi_ref[e] - lo
            size = tile_rows // SUBLANES // 2
            while size >= 1:
                @pl.when(jnp.bitwise_and(length, size) != 0)
                def _(size=size):
                    pos = lo + jnp.bitwise_and(length, -2 * size)
                    cp = pltpu.make_async_copy(
                        zero_ref.at[pl.ds(0, size * SUBLANES), :],
                        xb_ref.at[pl.ds(pl.multiple_of(pos * SUBLANES, SUBLANES), size * SUBLANES), :], zsem)
                    cp.wait() if wait else cp.start()
                size //= 2
            return 0
        lax.fori_loop(0, N_EXPERTS, pad_rows, 0)

        def tail(b, _):
            tail_tile(b).wait() if wait else tail_tile(b).start()
            return 0
        lax.fori_loop(nu_ref[0], n_tiles, tail, 0)

    def load(tile, slot):
        return pltpu.make_async_copy(h_ref.at[pl.ds(pl.multiple_of(tile * rows, rows), rows), :],
                                     hbuf_ref.at[pl.ds(pl.multiple_of(slot * rows, rows), rows), :],
                                     lsem.at[slot])

    def wait_scatter(slot):
        def wait(t, _):
            for k in range(TOP_K):
                _row_copy(hbuf_ref, 0, xb_ref, 0, sem.at[slot]).wait()
            return 0
        lax.fori_loop(0, tm, wait, 0, unroll=ROW_DMA_UNROLL)

    slot = i % DISPATCH_SLOTS
    nxt = (i + 1) % DISPATCH_SLOTS

    @pl.when(i == 0)
    def _():
        load(0, 0).start()
        zero_ref[...] = jnp.zeros(zero_ref.shape, F32)
        zero_fill(False)
        zero_fill(True)

    load(i, slot).wait()

    @pl.when(i >= DISPATCH_SLOTS - 1)
    def _():
        wait_scatter(nxt)

    @pl.when(i + 1 < n)
    def _():
        load(i + 1, nxt).start()

    def start(t, _):
        for k in range(TOP_K):
            _row_copy(hbuf_ref, slot * rows + t * SUBLANES, xb_ref, dst_ref[t * TOP_K + k],
                      sem.at[slot]).start(priority=k % 2)
        return 0

    lax.fori_loop(0, tm, start, 0, unroll=ROW_DMA_UNROLL)

    @pl.when(i == n - 1)
    def _():
        for back in range(min(DISPATCH_SLOTS - 1, s_tiles)):
            wait_scatter((i - back) % DISPATCH_SLOTS)


def _dispatch(dst8, pad_lo, pad_hi, n_used, h3, n_tiles, tile_tokens):
    s = h3.shape[0] // SUBLANES
    tm = min(DISPATCH_TM, s)
    smem = pl.BlockSpec(memory_space=pltpu.SMEM)
    body = functools.partial(_dispatch_kernel, s_tiles=s // tm)
    return pl.pallas_call(
        body,
        grid=(s // tm,),
        in_specs=[pl.BlockSpec((tm * TOP_K,), lambda i: (i,), memory_space=pltpu.SMEM), smem, smem, smem,
                  pl.BlockSpec(memory_space=pl.ANY)],
        out_specs=pl.BlockSpec(memory_space=pl.ANY),
        out_shape=jax.ShapeDtypeStruct((n_tiles * tile_tokens * SUBLANES, LANES), F32),
        scratch_shapes=[pltpu.VMEM((tile_tokens * SUBLANES, LANES), F32),
                        pltpu.VMEM((DISPATCH_SLOTS * tm * SUBLANES, LANES), F32),
                        pltpu.SemaphoreType.DMA((DISPATCH_SLOTS,)), pltpu.SemaphoreType.DMA(()),
                        pltpu.SemaphoreType.DMA((DISPATCH_SLOTS,))],
        compiler_params=pltpu.CompilerParams(dimension_semantics=("arbitrary",),
                                             vmem_limit_bytes=VMEM_LIMIT),
        name="dispatch",
    )(dst8, pad_lo, pad_hi, n_used, h3)


def _expert_kernel(te_ref, nu_ref, valid_ref, x_ref, w1_ref, b1_ref, w2_ref, b2_ref, y_ref, w1b_ref, w2b_ref):
    b = pl.program_id(0)
    tm = x_ref.shape[0] // SUBLANES

    @pl.when(b >= nu_ref[0])
    def _():
        y_ref[...] = jnp.zeros(y_ref.shape, y_ref.dtype)

    @pl.when(b < nu_ref[0])
    def _():
        prev = te_ref[jnp.maximum(b - 1, 0)]

        @pl.when((b == 0) | (te_ref[b] != prev))
        def _():
            chunk = 128
            def cast(r, _):
                rows = pl.ds(pl.multiple_of(r * chunk, chunk), chunk)
                w1b_ref[rows, :] = w1_ref[rows, :].astype(BF16)
                w2b_ref[rows, :] = w2_ref[rows, :].astype(BF16)
                return 0
            lax.fori_loop(0, D_MODEL // chunk, cast, 0)

        def ffn(rows):
            xe = _load_token_tiles(x_ref, 0, rows).astype(BF16)
            hcat = _dot(xe, w1b_ref[...]) + b1_ref[...]
            h_glu = jnp.minimum(hcat[:, :D_FF], SWIGLU_LIMIT)
            h_lin = jnp.clip(hcat[:, D_FF:], -SWIGLU_LIMIT, SWIGLU_LIMIT)
            act = h_glu * jax.nn.sigmoid(SWIGLU_ALPHA * h_glu) * (h_lin + 1.0)
            _store_token_tiles(y_ref, _dot(act.astype(BF16), w2b_ref[...]) + b2_ref[...])

        half = tm // 2

        @pl.when(valid_ref[b] > half)
        def _():
            ffn(tm)

        @pl.when(valid_ref[b] <= half)
        def _():
            ffn(half)
            y_ref[half * SUBLANES:, :] = jnp.zeros((half * SUBLANES, LANES), y_ref.dtype)


def _experts(tile_e, n_used, valid, xb, w1, b1, w2, b2):
    p = xb.shape[0] // SUBLANES
    tm = EXPERT_TM
    row_map = lambda b, te, nu, va: (jnp.minimum(b, nu[0] - 1), 0)
    exp_map = lambda b, te, nu, va: (te[b], 0, 0)
    return pl.pallas_call(
        _expert_kernel,
        grid_spec=pltpu.PrefetchScalarGridSpec(
            num_scalar_prefetch=3,
            grid=(p // tm,),
            in_specs=[pl.BlockSpec((tm * SUBLANES, LANES), row_map),
                      pl.BlockSpec((None, D_MODEL, 2 * D_FF), exp_map),
                      pl.BlockSpec((None, 1, 2 * D_FF), exp_map),
                      pl.BlockSpec((None, D_FF, D_MODEL), exp_map),
                      pl.BlockSpec((None, 1, D_MODEL), exp_map)],
            out_specs=pl.BlockSpec((tm * SUBLANES, LANES), lambda b, te, nu, va: (b, 0)),
            scratch_shapes=[pltpu.VMEM((D_MODEL, 2 * D_FF), BF16), pltpu.VMEM((D_FF, D_MODEL), BF16)]),
        out_shape=jax.ShapeDtypeStruct((p * SUBLANES, LANES), F32),
        compiler_params=pltpu.CompilerParams(dimension_semantics=("arbitrary",),
                                             vmem_limit_bytes=VMEM_LIMIT),
        name="experts",
    )(tile_e, n_used, valid, xb, w1, b1, w2, b2)


def _combine_kernel(cur_ref, nxt_ref, h_ref, rg_ref, g_ref, b_ref, yb_ref, o_ref, buf_ref, sem):
    i = pl.program_id(0)
    tm = h_ref.shape[0]
    slot_rows = TOP_K * tm * SUBLANES
    slot = i % 2

    def gather(src_ref, into):
        def start(t, _):
            for k in range(TOP_K):
                _row_copy(yb_ref, src_ref[t * TOP_K + k], buf_ref, into * slot_rows + (k * tm + t) * SUBLANES,
                          sem.at[into]).start(priority=k % 2)
            return 0
        lax.fori_loop(0, tm, start, 0, unroll=ROW_DMA_UNROLL)

    @pl.when(i == 0)
    def _():
        gather(cur_ref, 0)

    def wait(t, _):
        for k in range(TOP_K):
            _row_copy(yb_ref, 0, buf_ref, 0, sem.at[slot]).wait()
        return 0

    lax.fori_loop(0, tm, wait, 0, unroll=ROW_DMA_UNROLL)

    @pl.when(i + 1 < pl.num_programs(0))
    def _():
        gather(nxt_ref, 1 - slot)

    h = h_ref[...]
    rg = rg_ref[...]
    moe = None
    for k in range(TOP_K):
        t = rg[:, k:k + 1] * _load_token_tiles(buf_ref, slot * (TOP_K * tm) + k * tm, tm)
        moe = t if moe is None else moe + t
    o_ref[...] = _layer_norm(DEEPNORM_ALPHA * h + moe, g_ref[...], b_ref[...])


def _combine(dst8, h, rg, g2, b2, yb):
    s = h.shape[0]
    tm = min(COMBINE_TM, s)
    last = s // tm - 1
    full = lambda a: pl.BlockSpec(a.shape, lambda i: (0,) * a.ndim)
    return pl.pallas_call(
        _combine_kernel,
        grid=(s // tm,),
        in_specs=[pl.BlockSpec((tm * TOP_K,), lambda i: (i,), memory_space=pltpu.SMEM),
                  pl.BlockSpec((tm * TOP_K,), lambda i: (jnp.minimum(i + 1, last),), memory_space=pltpu.SMEM),
                  pl.BlockSpec((tm, D_MODEL), lambda i: (i, 0)),
                  pl.BlockSpec((tm, LANES), lambda i: (i, 0)),
                  full(g2), full(b2),
                  pl.BlockSpec(memory_space=pl.ANY)],
        out_specs=pl.BlockSpec((tm, D_MODEL), lambda i: (i, 0)),
        out_shape=jax.ShapeDtypeStruct((s, D_MODEL), F32),
        scratch_shapes=[pltpu.VMEM((2 * TOP_K * tm * SUBLANES, LANES), F32), pltpu.SemaphoreType.DMA((2,))],
        compiler_params=pltpu.CompilerParams(dimension_semantics=("arbitrary",),
                                             vmem_limit_bytes=VMEM_LIMIT),
        name="combine",
    )(dst8, dst8, h, rg, g2, b2, yb)


def kernel(x, mem, w_in, b_fgate, t5_bias, w_mem_kv, w_br_fox, w_br_dil, w_br_mem, w_out, ln1_g, ln1_b,
           w_router, b_router, w_exp_in, b_exp_in, w_exp_out, b_exp_out, ln2_g, ln2_b):
    bsz, s, d = x.shape
    assert bsz == 1 and d == D_MODEL and w_in.shape[0] == 1
    assert s % (DIL_CONFIGS[-1][1] * DBLK) == 0
    x2 = x[0]
    w_in0 = w_in[0]

    fg_lanes = np.array([FG_PAIR_STRIDE * (h // 2) + FG_PIECES * (h % 2) + j
                         for h in range(FOX_HEADS) for j in range(FG_PIECES)])
    fg_heads = np.repeat(np.arange(FOX_HEADS), FG_PIECES)
    w_fg = jnp.zeros((D_MODEL, LANES), F32).at[:, fg_lanes].set(w_in0[:, OFF_FOX_F + fg_heads])
    bf = jnp.zeros((1, LANES), F32).at[0, fg_lanes].set(b_fgate[0][fg_heads])
    w_all = jnp.concatenate([w_in0[:, OFF_FOX_QKV:OFF_FOX_QKV + 2 * FOX_W], w_in0[:, OFF_DIL_QKV:OFF_GATES],
                             w_fg], axis=1).astype(BF16)
    wvt = w_in0[:, OFF_FOX_QKV + 2 * FOX_W:OFF_FOX_F].T.astype(BF16)
    w_gates = w_in0[:, OFF_GATES:].astype(BF16)

    qk, ext, vt, o_mem, *dil_qkv, stats = _proj(x2, w_all, wvt, bf, mem[0].astype(BF16),
                                                w_mem_kv[0].astype(BF16))

    tq = min(FOX_TQ, s)
    assert min(PROJ_TM, s) == min(FOX_TK, tq)
    nblk = stats.shape[0]
    gate_lane = np.array([FG_PAIR_STRIDE * (h // 2) + FG_PIECES * (h % 2) for h in range(FOX_HEADS)])
    q_norm = jnp.sqrt(stats[:, 0, :FOX_HEADS, 0])
    k_norm = lax.cummax(jnp.sqrt(stats[:, 1, :FOX_HEADS, 0]), axis=0)
    neg_c = lax.cummax(stats[:, 2, 0, gate_lane], axis=0)
    q_norm = jnp.max(q_norm.reshape(s // tq, nblk * tq // s, FOX_HEADS), axis=1)
    o_fox = _fox(k_norm.T.reshape(-1), neg_c.T.reshape(-1), q_norm.T.reshape(-1), qk, ext, vt)

    ii = np.arange(DBLK, dtype=np.int32)[:, None]
    jj = np.arange(2 * DBLK, dtype=np.int32)[None, :]
    rel = np.clip(DBLK + ii - jj, 0, None)
    table_flat = t5_bias.reshape(-1)
    ods = []
    for g, (window, dil) in enumerate(DIL_CONFIGS):
        assert window // dil == DBLK
        bucket = _t5_bucket(jnp.asarray(rel * dil, dtype=jnp.int32))
        ods.append(_dil(g, table_flat, dil_qkv[g], bucket))

    wr = jnp.zeros((D_MODEL, LANES), F32).at[:, :N_EXPERTS].set(w_router[0]).astype(BF16)
    br = jnp.zeros((1, LANES), F32).at[0, :N_EXPERTS].set(b_router[0])
    h1, h3, ri, rg, cnt = _merge(x2, o_fox, ods, o_mem, w_gates, w_br_fox[0].astype(BF16),
                                 w_br_dil[0].astype(BF16), w_br_mem[0].astype(BF16), w_out[0].astype(BF16),
                                 ln1_g, ln1_b, wr, br)

    tm = EXPERT_TM
    counts = cnt[0, :N_EXPERTS].astype(jnp.int32)
    padded = (counts + tm - 1) // tm * tm
    pad_ends = jnp.cumsum(padded)
    pad_starts = (pad_ends - padded).astype(jnp.int32)
    n_tiles = (s * TOP_K) // tm + N_EXPERTS
    n_used = (pad_ends[-1] // tm).astype(jnp.int32)
    tile_lo = jnp.arange(n_tiles, dtype=jnp.int32) * tm
    tile_e = jnp.minimum(jnp.sum((pad_ends[None, :] <= tile_lo[:, None]).astype(jnp.int32), axis=1),
                         N_EXPERTS - 1)
    tile_e = tile_e[jnp.minimum(jnp.arange(n_tiles), n_used - 1)]
    tile_valid = jnp.clip((pad_starts + counts)[tile_e] - tile_lo, 0, tm).astype(jnp.int32)
    e_tk = ri[:, 0:TOP_K]
    start_tk = jnp.sum(jnp.where(e_tk[:, :, None] == jnp.arange(N_EXPERTS, dtype=jnp.int32),
                                 pad_starts, 0), axis=-1)
    dst8 = ((start_tk + ri[:, TOP_K:2 * TOP_K]) * SUBLANES).reshape(-1)

    xb = _dispatch(dst8, pad_starts + counts, pad_ends.astype(jnp.int32), n_used.reshape(1), h3, n_tiles, tm)
    yb = _experts(tile_e, n_used.reshape(1), tile_valid, xb, w_exp_in[0], b_exp_in[0][:, None, :], w_exp_out[0],
                  b_exp_out[0][:, None, :])
    out = _combine(dst8, h1, rg, ln2_g, ln2_b, yb)
    return out[None]
```

```python
import functools

import numpy as np
import jax
import jax.numpy as jnp
from jax import lax
from jax.experimental import pallas as pl
from jax.experimental.pallas import tpu as pltpu

D_MODEL = 1024
HEAD_DIM = 64
LANES = 128
SUBLANES = 8
assert D_MODEL == SUBLANES * LANES
FOX_HEADS = 6
FOX_PAIRS = FOX_HEADS // 2
DIL_CONFIGS = ((128, 1), (512, 4), (2048, 16))
DIL_GROUPS = len(DIL_CONFIGS)
DIL_HEADS = 2 * DIL_GROUPS
MEM_HEADS = 4
MEM_PAIRS = MEM_HEADS // 2
MEM_LEN = 256
FOX_W = FOX_HEADS * HEAD_DIM
DIL_W = DIL_HEADS * HEAD_DIM
MEM_W = MEM_HEADS * HEAD_DIM
DBLK = 128
DIL_QKV_W = 3 * LANES
T5_BUCKETS = 32
T5_MAX_EXACT = T5_BUCKETS // 2
T5_MAX_DISTANCE = 2048
N_EXPERTS = 32
TOP_K = 4
D_FF = D_MODEL
SWIGLU_LIMIT = 7.0
SWIGLU_ALPHA = 1.702
LN_EPS = 1e-5
DEEPNORM_ALPHA = 2.0 ** 0.25
QK_SCALE = HEAD_DIM ** -0.5
F32_EXP_ZERO_BELOW = 110.0

OFF_FOX_QKV = 0
OFF_FOX_F = OFF_FOX_QKV + 3 * FOX_W
OFF_DIL_QKV = OFF_FOX_F + FOX_HEADS
OFF_MEM_Q = OFF_DIL_QKV + 3 * DIL_W
OFF_GATES = OFF_MEM_Q + MEM_W

W_FOXQ = 0
W_FOXK = FOX_W
W_DIL = 2 * FOX_W
W_MEMQ = W_DIL + 3 * DIL_W
W_FGATE = W_MEMQ + MEM_W
W_ALL = W_FGATE + LANES
FG_PAIR_STRIDE = 8
FG_PIECES = 3
VT_ROWS = HEAD_DIM + 16

VMEM_LIMIT = 56 * 1024 * 1024

PROJ_TM = 512
FOX_TQ = 1024
FOX_TK = 512
FOX_TC = 1024
DIL_CHUNK = 512
MERGE_TM = 512
DISPATCH_TM = 512
DISPATCH_SLOTS = 3
EXPERT_TM = 512
COMBINE_TM = 512
ROW_DMA_UNROLL = 4

BF16 = jnp.bfloat16
F32 = jnp.float32


def _dot(a, b):
    return jnp.dot(a, b, preferred_element_type=F32)


def _dot_nt(a, b):
    return lax.dot_general(a, b, (((1,), (1,)), ((), ())), preferred_element_type=F32)


def _lane_lt(shape, n):
    return lax.broadcasted_iota(jnp.int32, shape, len(shape) - 1) < n


def _split3(v):
    hi = v.astype(BF16)
    r1 = v - hi.astype(F32)
    mid = r1.astype(BF16)
    lo = (r1 - mid.astype(F32)).astype(BF16)
    return hi, mid, lo


def _store_token_tiles(ref, v):
    n = v.shape[0]
    for c in range(SUBLANES):
        ref[pl.ds(c, n, stride=SUBLANES), :] = v[:, c * LANES:(c + 1) * LANES]


def _load_token_tiles(ref, lo, n):
    return jnp.concatenate([ref[pl.ds(lo * SUBLANES + c, n, stride=SUBLANES), :] for c in range(SUBLANES)],
                           axis=1)


def _layer_norm(r, g, b):
    mu = jnp.mean(r, axis=-1, keepdims=True)
    d = r - mu
    var = jnp.mean(d * d, axis=-1, keepdims=True)
    return d * lax.rsqrt(var + LN_EPS) * g + b


def _proj_kernel(x_ref, w_ref, wvt_ref, bf_ref, mem_ref, wkv_ref, qk_ref, ext_ref, vt_ref, omem_ref,
                 d0_ref, d1_ref, d2_ref, stat_ref, kvm_ref, carry_ref, stage_ref):
    i = pl.program_id(0)
    tm = x_ref.shape[0]

    @pl.when(i == 0)
    def _():
        kvm_ref[...] = _dot(mem_ref[...], wkv_ref[...]).astype(BF16)
        carry_ref[...] = jnp.zeros_like(carry_ref)

    xb = x_ref[...].astype(BF16)
    proj = _dot(xb, w_ref[...])

    qb = (proj[:, W_FOXQ:W_FOXK] * QK_SCALE).astype(BF16)
    kb = proj[:, W_FOXK:W_DIL].astype(BF16)
    qk_ref[:, W_FOXQ:W_FOXK] = qb
    qk_ref[:, W_FOXK:W_DIL] = kb

    stat_ref[...] = jnp.zeros(stat_ref.shape, F32)
    first_head = _lane_lt((tm, LANES), HEAD_DIM)
    for which, vb in enumerate((qb, kb)):
        for p in range(FOX_PAIRS):
            sq = jnp.square(vb[:, p * LANES:(p + 1) * LANES].astype(F32))
            for a in range(2):
                mask = first_head if a == 0 else jnp.logical_not(first_head)
                norm2 = jnp.sum(jnp.where(mask, sq, 0.0), axis=1, keepdims=True)
                top = jnp.max(norm2, axis=0, keepdims=True)
                stat_ref[0, which, 2 * p + a:2 * p + a + 1, :] = jnp.broadcast_to(top, (1, LANES))

    for g, (d_ref, (_, dil)) in enumerate(zip((d0_ref, d1_ref, d2_ref), DIL_CONFIGS)):
        cols = [proj[:, W_DIL + part * DIL_W + g * LANES:W_DIL + part * DIL_W + (g + 1) * LANES]
                for part in range(3)]
        cols[0] = cols[0] * QK_SCALE
        if dil == 1:
            for part in range(3):
                d_ref[0, :, part * LANES:(part + 1) * LANES] = cols[part].astype(BF16)
        else:
            for part in range(3):
                stage_ref[part] = cols[part]
            for r in range(dil):
                for part in range(3):
                    d_ref[r, :, part * LANES:(part + 1) * LANES] = (
                        stage_ref[part, pl.ds(r, tm // dil, stride=dil), :].astype(BF16))

    vt = _dot_nt(wvt_ref[...], xb)
    for h in range(FOX_HEADS):
        vt_ref[h, 0:HEAD_DIM, :] = vt[h * HEAD_DIM:(h + 1) * HEAD_DIM, :].astype(BF16)
        vt_ref[h, HEAD_DIM:VT_ROWS, :] = jnp.ones((VT_ROWS - HEAD_DIM, tm), BF16)

    z = proj[:, W_FGATE:W_ALL] + bf_ref[...]
    logf = jnp.minimum(z, 0.0) - jnp.log1p(jnp.exp(-jnp.abs(z)))
    row = lax.broadcasted_iota(jnp.int32, (tm, tm), 0)
    col = lax.broadcasted_iota(jnp.int32, (tm, tm), 1)
    lower = jnp.where(col <= row, 1.0, 0.0).astype(BF16)
    hi, mid, lo = _split3(logf)
    c = (_dot(lower, lo) + _dot(lower, mid)) + _dot(lower, hi) + carry_ref[0:1, :]
    carry_ref[...] = jnp.broadcast_to(c[tm - 1:tm, :], carry_ref.shape)
    stat_ref[0, 2, 0:1, :] = jnp.max(-c, axis=0, keepdims=True)
    nhi, nmid, nlo = _split3(-c)
    lane = lax.broadcasted_iota(jnp.int32, (tm, LANES), 1)
    slot = lane & (FG_PAIR_STRIDE - 1)
    piece = jnp.where(slot >= FG_PIECES, slot - FG_PIECES, slot)
    used = (slot < 2 * FG_PIECES) & (lane < FOX_PAIRS * FG_PAIR_STRIDE)
    ext = jnp.where(piece == 0, nhi, jnp.where(piece == 1, nmid, nlo))
    ext_ref[...] = jnp.where(used, ext, jnp.zeros_like(ext))

    qm = (proj[:, W_MEMQ:W_FGATE] * QK_SCALE).astype(BF16)
    for p in range(MEM_PAIRS):
        qp = qm[:, p * LANES:(p + 1) * LANES]
        kp = kvm_ref[:, p * LANES:(p + 1) * LANES]
        vp = kvm_ref[:, MEM_W + p * LANES:MEM_W + (p + 1) * LANES]
        first = _lane_lt(qp.shape, HEAD_DIM)
        outs = []
        for a in range(2):
            qa = jnp.where(first if a == 0 else jnp.logical_not(first), qp, jnp.zeros_like(qp))
            s = _dot_nt(qa, kp)
            m = jnp.max(s, axis=-1, keepdims=True)
            e = jnp.exp(s - m)
            den = jnp.sum(e, axis=-1, keepdims=True)
            outs.append(_dot(e.astype(BF16), vp) / den)
        omem_ref[:, p * LANES:(p + 1) * LANES] = jnp.where(first, outs[0], outs[1]).astype(BF16)


def _proj(x2, w_all, wvt, bf, mem_b, wkv_b):
    s = x2.shape[0]
    tm = min(PROJ_TM, s)
    full = lambda shape: pl.BlockSpec(shape, lambda i: (0,) * len(shape))
    rows = lambda w: pl.BlockSpec((tm, w), lambda i: (i, 0))
    dil_spec = lambda dil: pl.BlockSpec((dil, tm // dil, DIL_QKV_W), lambda i: (0, i, 0))
    dil_shape = lambda dil: jax.ShapeDtypeStruct((dil, s // dil, DIL_QKV_W), BF16)
    dils = [dil for _, dil in DIL_CONFIGS]
    return pl.pallas_call(
        _proj_kernel,
        grid=(s // tm,),
        in_specs=[rows(D_MODEL),
                  full(w_all.shape), full(wvt.shape), full(bf.shape), full(mem_b.shape), full(wkv_b.shape)],
        out_specs=[rows(W_DIL), rows(LANES),
                   pl.BlockSpec((FOX_HEADS, VT_ROWS, tm), lambda i: (0, 0, i)),
                   rows(MEM_W)] + [dil_spec(d) for d in dils] +
                  [pl.BlockSpec((1, 3, SUBLANES, LANES), lambda i: (i, 0, 0, 0))],
        out_shape=[jax.ShapeDtypeStruct((s, W_DIL), BF16),
                   jax.ShapeDtypeStruct((s, LANES), BF16),
                   jax.ShapeDtypeStruct((FOX_HEADS, VT_ROWS, s), BF16),
                   jax.ShapeDtypeStruct((s, MEM_W), BF16)] + [dil_shape(d) for d in dils] +
                  [jax.ShapeDtypeStruct((s // tm, 3, SUBLANES, LANES), F32)],
        scratch_shapes=[pltpu.VMEM((MEM_LEN, 2 * MEM_W), BF16), pltpu.VMEM((8, LANES), F32),
                        pltpu.VMEM((3, tm, LANES), F32)],
        compiler_params=pltpu.CompilerParams(dimension_semantics=("arbitrary",),
                                             vmem_limit_bytes=VMEM_LIMIT),
        name="proj",
    )(x2, w_all, wvt, bf, mem_b, wkv_b)


def _fox_kernel(km_ref, nc_ref, qn_ref, q_ref, k_ref, e_ref, vt_ref, o_ref, m_ref, acc_ref, qx_ref, s_ref):
    p = pl.program_id(0)
    i = pl.program_id(1)
    t = q_ref.shape[0]
    tk = min(FOX_TK, t)
    q = q_ref[...]
    lane = lax.broadcasted_iota(jnp.int32, (t, LANES), 1)
    for a in range(2):
        head = (lane < HEAD_DIM) if a == 0 else (lane >= HEAD_DIM)
        lo = FG_PAIR_STRIDE * p + FG_PIECES * a
        ones = jnp.where((lane >= lo) & (lane < lo + FG_PIECES), 1.0, 0.0).astype(BF16)
        qx_ref[a, :, 0:LANES] = jnp.where(head, q, jnp.zeros_like(q))
        qx_ref[a, :, LANES:2 * LANES] = ones

    m_ref[...] = jnp.full(m_ref.shape, -jnp.inf, F32)
    acc_ref[...] = jnp.zeros(acc_ref.shape, F32)

    tc = min(FOX_TC, t)

    def phase(nxt, cur):
        if nxt is not None:
            nblk, nq0, nbuf = nxt
            nstart = pl.multiple_of(nblk * tk, tk)
            kx = jnp.concatenate([k_ref[pl.ds(nstart, tk), :], e_ref[pl.ds(nstart, tk), :]], axis=1)
        if cur is not None:
            cblk, cq0, cbuf, diagonal = cur
            cstart = pl.multiple_of(cblk * tk, tk)
            vtb = [vt_ref[a, :, pl.ds(cstart, tk)] for a in range(2)]
        for c in range(0, t, tc):
            if nxt is not None and c + tc > nq0:
                cols = slice(max(c, nq0), c + tc)
                for a in range(2):
                    s_ref[nbuf, a, :, cols] = _dot_nt(kx, qx_ref[a, cols, :])
            if cur is not None and c + tc > cq0:
                cols = slice(max(c, cq0), c + tc)
                for a in range(2):
                    s = s_ref[cbuf, a, :, cols]
                    if diagonal:
                        kpos = lax.broadcasted_iota(jnp.int32, s.shape, 0) + cq0
                        qpos = lax.broadcasted_iota(jnp.int32, s.shape, 1) + cols.start
                        s = jnp.where(kpos <= qpos, s, -jnp.inf)
                    m_prev = m_ref[a, :, cols]
                    m_new = jnp.maximum(m_prev, jnp.max(s, axis=0, keepdims=True))
                    alpha = jnp.exp(m_prev - m_new)
                    pt = jnp.exp(s - m_new).astype(BF16)
                    m_ref[a, :, cols] = m_new
                    acc_ref[a, :, cols] = alpha * acc_ref[a, :, cols] + _dot(vtb[a], pt)

    nd = t // tk
    nfull = i * nd
    phase((nfull, 0, 0), None)
    for d in range(nd):
        nxt = (nfull + d + 1, (d + 1) * tk, (d + 1) % 2) if d + 1 < nd else None
        phase(nxt, (nfull + d, d * tk, d % 2, True))

    nblocks = k_ref.shape[0] // tk
    nq = pl.num_programs(1)
    need = jnp.int32(0)
    for a in range(2):
        h = 2 * p + a
        thr = jnp.min(m_ref[a]) - F32_EXP_ZERO_BELOW
        qn = qn_ref[h * nq + i]
        count = lax.fori_loop(
            0, nfull,
            lambda j, n: n + (qn * km_ref[h * nblocks + j] + nc_ref[h * nblocks + j] >= thr).astype(jnp.int32),
            jnp.int32(0))
        need = jnp.maximum(need, count)
    need = jnp.minimum(jnp.bitwise_and(need + 1, -2), nfull)
    first = nfull - need

    @pl.when(need > 0)
    def _():
        phase((first, 0, 0), None)

        def pair(j, _):
            b0 = first + 2 * j
            phase((b0 + 1, 0, 1), (b0, 0, 0, False))
            phase((b0 + 2, 0, 0), (b0 + 1, 0, 1, False))
            return 0

        lax.fori_loop(0, need // 2 - 1, pair, 0)
        phase((nfull - 1, 0, 1), (nfull - 2, 0, 0, False))
        phase(None, (nfull - 1, 0, 1, False))
    o0 = acc_ref[0, 0:HEAD_DIM, :] / acc_ref[0, HEAD_DIM:HEAD_DIM + 1, :]
    o1 = acc_ref[1, 0:HEAD_DIM, :] / acc_ref[1, HEAD_DIM:HEAD_DIM + 1, :]
    o_ref[...] = jnp.concatenate([o0, o1], axis=0).T.astype(BF16)


def _fox(km, nc, qn, qk, ext, vt):
    s = qk.shape[0]
    t = min(FOX_TQ, s)
    tk = min(FOX_TK, t)
    assert s % t == 0 and t % (2 * tk) == 0
    kcol = W_FOXK // LANES
    return pl.pallas_call(
        _fox_kernel,
        grid=(FOX_PAIRS, s // t),
        in_specs=[pl.BlockSpec(memory_space=pltpu.SMEM), pl.BlockSpec(memory_space=pltpu.SMEM),
                  pl.BlockSpec(memory_space=pltpu.SMEM),
                  pl.BlockSpec((t, LANES), lambda p, i: (i, p)),
                  pl.BlockSpec((s, LANES), lambda p, i: (0, kcol + p)),
                  pl.BlockSpec((s, LANES), lambda p, i: (0, 0)),
                  pl.BlockSpec((2, VT_ROWS, s), lambda p, i: (p, 0, 0))],
        out_specs=pl.BlockSpec((t, LANES), lambda p, i: (i, p)),
        out_shape=jax.ShapeDtypeStruct((s, FOX_W), BF16),
        scratch_shapes=[pltpu.VMEM((2, 1, t), F32), pltpu.VMEM((2, VT_ROWS, t), F32),
                        pltpu.VMEM((2, t, 2 * LANES), BF16), pltpu.VMEM((2, 2, tk, t), F32)],
        compiler_params=pltpu.CompilerParams(dimension_semantics=("arbitrary", "arbitrary"),
                                             vmem_limit_bytes=VMEM_LIMIT),
        name="fox",
    )(km, nc, qn, qk, qk, ext, vt)


def _dil_kernel(table_ref, cur_ref, prev_ref, bucket_ref, o_ref, bias_ref, *, group, blocks_per_seq):
    j = pl.program_id(0)
    nblk = cur_ref.shape[0] // DBLK
    ii = lax.broadcasted_iota(jnp.int32, (DBLK, 2 * DBLK), 0)
    jj = lax.broadcasted_iota(jnp.int32, (DBLK, 2 * DBLK), 1)

    @pl.when(j == 0)
    def _():
        bucket = bucket_ref[...]
        rel = DBLK + ii - jj
        in_window = (rel >= 0) & (rel <= DBLK)
        for a in range(2):
            b = jnp.zeros(bucket.shape, F32)
            for t in range(T5_BUCKETS):
                b = jnp.where(bucket == t, table_ref[t * DIL_HEADS + 2 * group + a], b)
            bias_ref[a] = jnp.where(in_window, b, -jnp.inf)

    first = _lane_lt((DBLK, LANES), HEAD_DIM)
    chunk_has_prev = ((j * nblk) % blocks_per_seq) != 0
    for b in range(nblk):
        rows = slice(b * DBLK, (b + 1) * DBLK)
        q = cur_ref[rows, 0:LANES]
        if b == 0:
            kprev, vprev = prev_ref[:, LANES:2 * LANES], prev_ref[:, 2 * LANES:3 * LANES]
        else:
            before = slice((b - 1) * DBLK, b * DBLK)
            kprev, vprev = cur_ref[before, LANES:2 * LANES], cur_ref[before, 2 * LANES:3 * LANES]
        kcat = jnp.concatenate([kprev, cur_ref[rows, LANES:2 * LANES]], axis=0)
        vcat = jnp.concatenate([vprev, cur_ref[rows, 2 * LANES:3 * LANES]], axis=0)
        outs, lses = [], []
        for a in range(2):
            qa = jnp.where(first if a == 0 else jnp.logical_not(first), q, jnp.zeros_like(q))
            s = _dot_nt(qa, kcat) + bias_ref[a]
            if b == 0:
                s = jnp.where(chunk_has_prev | (jj >= DBLK), s, -jnp.inf)
            m = jnp.max(s, axis=-1, keepdims=True)
            e = jnp.exp(s - m)
            den = jnp.sum(e, axis=-1, keepdims=True)
            outs.append(_dot(e.astype(BF16), vcat) / den)
            lses.append(jnp.broadcast_to(m + jnp.log(den), (DBLK, LANES)))
        o_ref[rows, 0:LANES] = jnp.where(first, outs[0], outs[1])
        o_ref[rows, LANES:2 * LANES] = jnp.where(first, lses[0], lses[1])


def _dil(group, dil, table_flat, qkv_sub, bucket):
    s = qkv_sub.shape[0]
    seq = s // dil
    chunk = min(DIL_CHUNK, seq)
    assert seq % chunk == 0 and chunk % DBLK == 0
    nblk = chunk // DBLK
    body = functools.partial(_dil_kernel, group=group, blocks_per_seq=seq // DBLK)
    return pl.pallas_call(
        body,
        grid=(s // chunk,),
        in_specs=[pl.BlockSpec(memory_space=pltpu.SMEM),
                  pl.BlockSpec((chunk, DIL_QKV_W), lambda j: (j, 0)),
                  pl.BlockSpec((DBLK, DIL_QKV_W), lambda j: (jnp.maximum(j * nblk - 1, 0), 0)),
                  pl.BlockSpec((DBLK, 2 * DBLK), lambda j: (0, 0))],
        out_specs=pl.BlockSpec((chunk, 2 * LANES), lambda j: (j, 0)),
        out_shape=jax.ShapeDtypeStruct((s, 2 * LANES), F32),
        scratch_shapes=[pltpu.VMEM((2, DBLK, 2 * DBLK), F32)],
        compiler_params=pltpu.CompilerParams(dimension_semantics=("arbitrary",),
                                             vmem_limit_bytes=VMEM_LIMIT),
        name=f"dil{group}",
    )(table_flat, qkv_sub, qkv_sub, bucket)


def _t5_bucket(dist):
    is_small = dist < T5_MAX_EXACT
    nf = jnp.maximum(dist, T5_MAX_EXACT).astype(F32)
    large = T5_MAX_EXACT + (jnp.log(nf / T5_MAX_EXACT) / np.log(T5_MAX_DISTANCE / T5_MAX_EXACT)
                            * (T5_BUCKETS - T5_MAX_EXACT)).astype(jnp.int32)
    large = jnp.minimum(large, T5_BUCKETS - 1)
    return jnp.where(is_small, dist, large)


def _merge_kernel(x_ref, ofox_ref, od0_ref, od1_ref, od2_ref, omem_ref, wg_ref, wbf_ref, wbd_ref, wbm_ref,
                  wout_ref, g_ref, b_ref, wr_ref, br_ref, h_ref, h3_ref, ri_ref, rg_ref, cnt_ref,
                  carry_ref, tok_ref):
    i = pl.program_id(0)
    tm = x_ref.shape[0]

    @pl.when(i == 0)
    def _():
        carry_ref[...] = jnp.zeros_like(carry_ref)

    x = x_ref[...]
    xb = x.astype(BF16)

    def gate(br):
        return jax.nn.sigmoid(_dot(xb, wg_ref[:, br * D_MODEL:(br + 1) * D_MODEL]))

    merged = gate(0) * _dot(ofox_ref[...], wbf_ref[...])

    for g, (od_ref, (_, dil)) in enumerate(zip((od0_ref, od1_ref, od2_ref), DIL_CONFIGS)):
        for r in range(dil):
            for half in range(2):
                tok_ref[g, half, pl.ds(r, tm // dil, stride=dil), :] = od_ref[r, :, half * LANES:(half + 1) * LANES]
    lse = [tok_ref[g, 1] for g in range(DIL_GROUPS)]
    mx = jnp.maximum(jnp.maximum(lse[0], lse[1]), lse[2])
    ex = [jnp.exp(l - mx) for l in lse]
    den = ex[0] + ex[1] + ex[2]
    o_dil = jnp.concatenate([(tok_ref[g, 0] * (ex[g] / den)).astype(BF16) for g in range(DIL_GROUPS)], axis=1)
    merged = merged + gate(1) * _dot(o_dil, wbd_ref[...])
    merged = merged + gate(2) * _dot(omem_ref[...], wbm_ref[...])

    y = _dot(merged.astype(BF16), wout_ref[...])
    h = _layer_norm(DEEPNORM_ALPHA * x + y, g_ref[...], b_ref[...])
    h_ref[...] = h
    _store_token_tiles(h3_ref, h)

    lane = lax.broadcasted_iota(jnp.int32, (tm, LANES), 1)
    logits = _dot(h.astype(BF16), wr_ref[...]) + br_ref[...]
    vals = jnp.where(lane < N_EXPERTS, logits, -jnp.inf)
    tops, hots = [], []
    for _ in range(TOP_K):
        mk = jnp.max(vals, axis=-1, keepdims=True)
        ik = jnp.min(jnp.where(vals == mk, lane, LANES), axis=-1, keepdims=True)
        hot = lane == ik
        vals = jnp.where(hot, -jnp.inf, vals)
        tops.append((mk, ik))
        hots.append(hot)
    es = [jnp.exp(mk - tops[0][0]) for mk, _ in tops]
    esum = es[0] + es[1] + es[2] + es[3]

    picked = jnp.where(hots[0] | hots[1] | hots[2] | hots[3], 1.0, 0.0)
    row = lax.broadcasted_iota(jnp.int32, (tm, tm), 0)
    col = lax.broadcasted_iota(jnp.int32, (tm, tm), 1)
    lower = jnp.where(col < row, 1.0, 0.0).astype(BF16)
    before = _dot(lower, picked.astype(BF16)) + carry_ref[0:1, :]
    ri = jnp.zeros((tm, LANES), jnp.int32)
    rg = jnp.zeros((tm, LANES), F32)
    for k in range(TOP_K):
        rank = jnp.sum(jnp.where(hots[k], before, 0.0), axis=-1, keepdims=True)
        ri = jnp.where(lane == k, tops[k][1], ri)
        ri = jnp.where(lane == TOP_K + k, rank.astype(jnp.int32), ri)
        rg = jnp.where(lane == k, es[k] / esum, rg)
    ri_ref[...] = ri
    rg_ref[...] = rg
    total = carry_ref[...] + jnp.sum(picked, axis=0, keepdims=True)
    carry_ref[...] = total
    cnt_ref[...] = total


def _merge(x2, ofox, ods, omem, wg, wbf, wbd, wbm, wout, g1, b1, wr, br):
    s = x2.shape[0]
    tm = min(MERGE_TM, s)
    full = lambda a: pl.BlockSpec(a.shape, lambda i: (0,) * a.ndim)
    rows = lambda w: pl.BlockSpec((tm, w), lambda i: (i, 0))
    od_specs = [pl.BlockSpec((dil, tm // dil, 2 * LANES), lambda i: (0, i, 0)) for _, dil in DIL_CONFIGS]
    return pl.pallas_call(
        _merge_kernel,
        grid=(s // tm,),
        in_specs=[rows(D_MODEL), rows(FOX_W)] + od_specs +
                 [rows(MEM_W), full(wg), full(wbf), full(wbd), full(wbm), full(wout),
                  full(g1), full(b1), full(wr), full(br)],
        out_specs=[rows(D_MODEL), pl.BlockSpec((tm * SUBLANES, LANES), lambda i: (i, 0)),
                   rows(LANES), rows(LANES), pl.BlockSpec((8, LANES), lambda i: (0, 0))],
        out_shape=[jax.ShapeDtypeStruct((s, D_MODEL), F32),
                   jax.ShapeDtypeStruct((s * SUBLANES, LANES), F32),
                   jax.ShapeDtypeStruct((s, LANES), jnp.int32),
                   jax.ShapeDtypeStruct((s, LANES), F32),
                   jax.ShapeDtypeStruct((8, LANES), F32)],
        scratch_shapes=[pltpu.VMEM((8, LANES), F32), pltpu.VMEM((DIL_GROUPS, 2, tm, LANES), F32)],
        compiler_params=pltpu.CompilerParams(dimension_semantics=("arbitrary",),
                                             vmem_limit_bytes=VMEM_LIMIT),
        name="merge",
    )(x2, ofox, *ods, omem, wg, wbf, wbd, wbm, wout, g1, b1, wr, br)


def _row_copy(src_ref, src_off, dst_ref, dst_off, sem):
    tile = lambda ref, off: ref.at[pl.ds(pl.multiple_of(off, SUBLANES), SUBLANES), :]
    return pltpu.make_async_copy(tile(src_ref, src_off), tile(dst_ref, dst_off), sem)


def _dispatch_kernel(dst_ref, lo_ref, hi_ref, nu_ref, h_ref, xb_ref, zero_ref, hbuf_ref, sem, zsem, lsem, *,
                     s_tiles):
    i = pl.program_id(0)
    n = pl.num_programs(0)
    rows = hbuf_ref.shape[0] // DISPATCH_SLOTS
    tm = rows // SUBLANES
    tile_rows = zero_ref.shape[0]
    n_tiles = xb_ref.shape[0] // tile_rows

    def tail_tile(b):
        return pltpu.make_async_copy(
            zero_ref, xb_ref.at[pl.ds(pl.multiple_of(b * tile_rows, tile_rows), tile_rows), :], zsem)

    def zero_fill(wait):
        def pad_rows(e, _):
            lo = lo_ref[e]
            length = hi_ref[e] - lo
            size = tile_rows // SUBLANES // 2
            while size >= 1:
                @pl.when(jnp.bitwise_and(length, size) != 0)
                def _(size=size):
                    pos = lo + jnp.bitwise_and(length, -2 * size)
                    cp = pltpu.make_async_copy(
                        zero_ref.at[pl.ds(0, size * SUBLANES), :],
                        xb_ref.at[pl.ds(pl.multiple_of(pos * SUBLANES, SUBLANES), size * SUBLANES), :], zsem)
                    cp.wait() if wait else cp.start()
                size //= 2
            return 0
        lax.fori_loop(0, N_EXPERTS, pad_rows, 0)

        def tail(b, _):
            tail_tile(b).wait() if wait else tail_tile(b).start()
            return 0
        lax.fori_loop(nu_ref[0], n_tiles, tail, 0)

    def load(tile, slot):
        return pltpu.make_async_copy(h_ref.at[pl.ds(pl.multiple_of(tile * rows, rows), rows), :],
                                     hbuf_ref.at[pl.ds(pl.multiple_of(slot * rows, rows), rows), :],
                                     lsem.at[slot])

    def wait_scatter(slot):
        def wait(t, _):
            for k in range(TOP_K):
                _row_copy(hbuf_ref, 0, xb_ref, 0, sem.at[slot]).wait()
            return 0
        lax.fori_loop(0, tm, wait, 0, unroll=ROW_DMA_UNROLL)

    slot = i % DISPATCH_SLOTS
    nxt = (i + 1) % DISPATCH_SLOTS

    @pl.when(i == 0)
    def _():
        load(0, 0).start()
        zero_ref[...] = jnp.zeros(zero_ref.shape, F32)
        zero_fill(False)
        zero_fill(True)

    load(i, slot).wait()

    @pl.when(i >= DISPATCH_SLOTS - 1)
    def _():
        wait_scatter(nxt)

    @pl.when(i + 1 < n)
    def _():
        load(i + 1, nxt).start()

    def start(t, _):
        for k in range(TOP_K):
            _row_copy(hbuf_ref, slot * rows + t * SUBLANES, xb_ref, dst_ref[t * TOP_K + k],
                      sem.at[slot]).start(priority=k % 2)
        return 0

    lax.fori_loop(0, tm, start, 0, unroll=ROW_DMA_UNROLL)

    @pl.when(i == n - 1)
    def _():
        for back in range(min(DISPATCH_SLOTS - 1, s_tiles)):
            wait_scatter((i - back) % DISPATCH_SLOTS)


def _dispatch(dst8, pad_lo, pad_hi, n_used, h3, n_tiles, tile_tokens):
    s = h3.shape[0] // SUBLANES
    tm = min(DISPATCH_TM, s)
    smem = pl.BlockSpec(memory_space=pltpu.SMEM)
    body = functools.partial(_dispatch_kernel, s_tiles=s // tm)
    return pl.pallas_call(
        body,
        grid=(s // tm,),
        in_specs=[pl.BlockSpec((tm * TOP_K,), lambda i: (i,), memory_space=pltpu.SMEM), smem, smem, smem,
                  pl.BlockSpec(memory_space=pl.ANY)],
        out_specs=pl.BlockSpec(memory_space=pl.ANY),
        out_shape=jax.ShapeDtypeStruct((n_tiles * tile_tokens * SUBLANES, LANES), F32),
        scratch_shapes=[pltpu.VMEM((tile_tokens * SUBLANES, LANES), F32),
                        pltpu.VMEM((DISPATCH_SLOTS * tm * SUBLANES, LANES), F32),
                        pltpu.SemaphoreType.DMA((DISPATCH_SLOTS,)), pltpu.SemaphoreType.DMA(()),
                        pltpu.SemaphoreType.DMA((DISPATCH_SLOTS,))],
        compiler_params=pltpu.CompilerParams(dimension_semantics=("arbitrary",),
                                             vmem_limit_bytes=VMEM_LIMIT),
        name="dispatch",
    )(dst8, pad_lo, pad_hi, n_used, h3)


def _expert_kernel(te_ref, nu_ref, x_ref, w1_ref, b1_ref, w2_ref, b2_ref, y_ref, w1b_ref, w2b_ref):
    b = pl.program_id(0)

    @pl.when(b >= nu_ref[0])
    def _():
        y_ref[...] = jnp.zeros(y_ref.shape, y_ref.dtype)

    @pl.when(b < nu_ref[0])
    def _():
        prev = te_ref[jnp.maximum(b - 1, 0)]

        @pl.when((b == 0) | (te_ref[b] != prev))
        def _():
            chunk = 128
            def cast(r, _):
                rows = pl.ds(pl.multiple_of(r * chunk, chunk), chunk)
                w1b_ref[rows, :] = w1_ref[rows, :].astype(BF16)
                w2b_ref[rows, :] = w2_ref[rows, :].astype(BF16)
                return 0
            lax.fori_loop(0, D_MODEL // chunk, cast, 0)

        xe = _load_token_tiles(x_ref, 0, x_ref.shape[0] // SUBLANES).astype(BF16)
        hcat = _dot(xe, w1b_ref[...]) + b1_ref[...]
        h_glu = jnp.minimum(hcat[:, :D_FF], SWIGLU_LIMIT)
        h_lin = jnp.clip(hcat[:, D_FF:], -SWIGLU_LIMIT, SWIGLU_LIMIT)
        act = h_glu * jax.nn.sigmoid(SWIGLU_ALPHA * h_glu) * (h_lin + 1.0)
        _store_token_tiles(y_ref, _dot(act.astype(BF16), w2b_ref[...]) + b2_ref[...])


def _experts(tile_e, n_used, xb, w1, b1, w2, b2):
    p = xb.shape[0] // SUBLANES
    tm = EXPERT_TM
    row_map = lambda b, te, nu: (jnp.minimum(b, nu[0] - 1), 0)
    exp_map = lambda b, te, nu: (te[b], 0, 0)
    return pl.pallas_call(
        _expert_kernel,
        grid_spec=pltpu.PrefetchScalarGridSpec(
            num_scalar_prefetch=2,
            grid=(p // tm,),
            in_specs=[pl.BlockSpec((tm * SUBLANES, LANES), row_map),
                      pl.BlockSpec((None, D_MODEL, 2 * D_FF), exp_map),
                      pl.BlockSpec((None, 1, 2 * D_FF), exp_map),
                      pl.BlockSpec((None, D_FF, D_MODEL), exp_map),
                      pl.BlockSpec((None, 1, D_MODEL), exp_map)],
            out_specs=pl.BlockSpec((tm * SUBLANES, LANES), lambda b, te, nu: (b, 0)),
            scratch_shapes=[pltpu.VMEM((D_MODEL, 2 * D_FF), BF16), pltpu.VMEM((D_FF, D_MODEL), BF16)]),
        out_shape=jax.ShapeDtypeStruct((p * SUBLANES, LANES), F32),
        compiler_params=pltpu.CompilerParams(dimension_semantics=("arbitrary",),
                                             vmem_limit_bytes=VMEM_LIMIT),
        name="experts",
    )(tile_e, n_used, xb, w1, b1, w2, b2)


def _combine_kernel(cur_ref, nxt_ref, h_ref, rg_ref, g_ref, b_ref, yb_ref, o_ref, buf_ref, sem):
    i = pl.program_id(0)
    tm = h_ref.shape[0]
    slot_rows = TOP_K * tm * SUBLANES
    slot = i % 2

    def gather(src_ref, into):
        def start(t, _):
            for k in range(TOP_K):
                _row_copy(yb_ref, src_ref[t * TOP_K + k], buf_ref, into * slot_rows + (k * tm + t) * SUBLANES,
                          sem.at[into]).start(priority=k % 2)
            return 0
        lax.fori_loop(0, tm, start, 0, unroll=ROW_DMA_UNROLL)

    @pl.when(i == 0)
    def _():
        gather(cur_ref, 0)

    def wait(t, _):
        for k in range(TOP_K):
            _row_copy(yb_ref, 0, buf_ref, 0, sem.at[slot]).wait()
        return 0

    lax.fori_loop(0, tm, wait, 0, unroll=ROW_DMA_UNROLL)

    @pl.when(i + 1 < pl.num_programs(0))
    def _():
        gather(nxt_ref, 1 - slot)

    h = h_ref[...]
    rg = rg_ref[...]
    moe = None
    for k in range(TOP_K):
        t = rg[:, k:k + 1] * _load_token_tiles(buf_ref, slot * (TOP_K * tm) + k * tm, tm)
        moe = t if moe is None else moe + t
    o_ref[...] = _layer_norm(DEEPNORM_ALPHA * h + moe, g_ref[...], b_ref[...])


def _combine(dst8, h, rg, g2, b2, yb):
    s = h.shape[0]
    tm = min(COMBINE_TM, s)
    last = s // tm - 1
    full = lambda a: pl.BlockSpec(a.shape, lambda i: (0,) * a.ndim)
    return pl.pallas_call(
        _combine_kernel,
        grid=(s // tm,),
        in_specs=[pl.BlockSpec((tm * TOP_K,), lambda i: (i,), memory_space=pltpu.SMEM),
                  pl.BlockSpec((tm * TOP_K,), lambda i: (jnp.minimum(i + 1, last),), memory_space=pltpu.SMEM),
                  pl.BlockSpec((tm, D_MODEL), lambda i: (i, 0)),
                  pl.BlockSpec((tm, LANES), lambda i: (i, 0)),
                  full(g2), full(b2),
                  pl.BlockSpec(memory_space=pl.ANY)],
        out_specs=pl.BlockSpec((tm, D_MODEL), lambda i: (i, 0)),
        out_shape=jax.ShapeDtypeStruct((s, D_MODEL), F32),
        scratch_shapes=[pltpu.VMEM((2 * TOP_K * tm * SUBLANES, LANES), F32), pltpu.SemaphoreType.DMA((2,))],
        compiler_params=pltpu.CompilerParams(dimension_semantics=("arbitrary",),
                                             vmem_limit_bytes=VMEM_LIMIT),
        name="combine",
    )(dst8, dst8, h, rg, g2, b2, yb)


def kernel(x, mem, w_in, b_fgate, t5_bias, w_mem_kv, w_br_fox, w_br_dil, w_br_mem, w_out, ln1_g, ln1_b,
           w_router, b_router, w_exp_in, b_exp_in, w_exp_out, b_exp_out, ln2_g, ln2_b):
    bsz, s, d = x.shape
    assert bsz == 1 and d == D_MODEL and w_in.shape[0] == 1
    assert s % (DIL_CONFIGS[-1][1] * DBLK) == 0
    x2 = x[0]
    w_in0 = w_in[0]

    fg_lanes = np.array([FG_PAIR_STRIDE * (h // 2) + FG_PIECES * (h % 2) + j
                         for h in range(FOX_HEADS) for j in range(FG_PIECES)])
    fg_heads = np.repeat(np.arange(FOX_HEADS), FG_PIECES)
    w_fg = jnp.zeros((D_MODEL, LANES), F32).at[:, fg_lanes].set(w_in0[:, OFF_FOX_F + fg_heads])
    bf = jnp.zeros((1, LANES), F32).at[0, fg_lanes].set(b_fgate[0][fg_heads])
    w_all = jnp.concatenate([w_in0[:, OFF_FOX_QKV:OFF_FOX_QKV + 2 * FOX_W], w_in0[:, OFF_DIL_QKV:OFF_GATES],
                             w_fg], axis=1).astype(BF16)
    wvt = w_in0[:, OFF_FOX_QKV + 2 * FOX_W:OFF_FOX_F].T.astype(BF16)
    w_gates = w_in0[:, OFF_GATES:].astype(BF16)

    qk, ext, vt, o_mem, *dil_qkv, stats = _proj(x2, w_all, wvt, bf, mem[0].astype(BF16),
                                                w_mem_kv[0].astype(BF16))

    tq = min(FOX_TQ, s)
    assert min(PROJ_TM, s) == min(FOX_TK, tq)
    nblk = stats.shape[0]
    gate_lane = np.array([FG_PAIR_STRIDE * (h // 2) + FG_PIECES * (h % 2) for h in range(FOX_HEADS)])
    q_norm = jnp.sqrt(stats[:, 0, :FOX_HEADS, 0])
    k_norm = lax.cummax(jnp.sqrt(stats[:, 1, :FOX_HEADS, 0]), axis=0)
    neg_c = lax.cummax(stats[:, 2, 0, gate_lane], axis=0)
    q_norm = jnp.max(q_norm.reshape(s // tq, nblk * tq // s, FOX_HEADS), axis=1)
    o_fox = _fox(k_norm.T.reshape(-1), neg_c.T.reshape(-1), q_norm.T.reshape(-1), qk, ext, vt)

    ii = np.arange(DBLK, dtype=np.int32)[:, None]
    jj = np.arange(2 * DBLK, dtype=np.int32)[None, :]
    rel = np.clip(DBLK + ii - jj, 0, None)
    table_flat = t5_bias.reshape(-1)
    ods = []
    for g, (window, dil) in enumerate(DIL_CONFIGS):
        assert window // dil == DBLK
        bucket = _t5_bucket(jnp.asarray(rel * dil, dtype=jnp.int32))
        od = _dil(g, dil, table_flat, dil_qkv[g].reshape(s, DIL_QKV_W), bucket)
        ods.append(od.reshape(dil, s // dil, 2 * LANES))

    wr = jnp.zeros((D_MODEL, LANES), F32).at[:, :N_EXPERTS].set(w_router[0]).astype(BF16)
    br = jnp.zeros((1, LANES), F32).at[0, :N_EXPERTS].set(b_router[0])
    h1, h3, ri, rg, cnt = _merge(x2, o_fox, ods, o_mem, w_gates, w_br_fox[0].astype(BF16),
                                 w_br_dil[0].astype(BF16), w_br_mem[0].astype(BF16), w_out[0].astype(BF16),
                                 ln1_g, ln1_b, wr, br)

    tm = EXPERT_TM
    counts = cnt[0, :N_EXPERTS].astype(jnp.int32)
    padded = (counts + tm - 1) // tm * tm
    pad_ends = jnp.cumsum(padded)
    pad_starts = (pad_ends - padded).astype(jnp.int32)
    n_tiles = (s * TOP_K) // tm + N_EXPERTS
    n_used = (pad_ends[-1] // tm).astype(jnp.int32)
    tile_lo = jnp.arange(n_tiles, dtype=jnp.int32) * tm
    tile_e = jnp.minimum(jnp.sum((pad_ends[None, :] <= tile_lo[:, None]).astype(jnp.int32), axis=1),
                         N_EXPERTS - 1)
    tile_e = tile_e[jnp.minimum(jnp.arange(n_tiles), n_used - 1)]
    e_tk = ri[:, 0:TOP_K]
    start_tk = jnp.sum(jnp.where(e_tk[:, :, None] == jnp.arange(N_EXPERTS, dtype=jnp.int32),
                                 pad_starts, 0), axis=-1)
    dst8 = ((start_tk + ri[:, TOP_K:2 * TOP_K]) * SUBLANES).reshape(-1)

    xb = _dispatch(dst8, pad_starts + counts, pad_ends.astype(jnp.int32), n_used.reshape(1), h3, n_tiles, tm)
    yb = _experts(tile_e, n_used.reshape(1), xb, w_exp_in[0], b_exp_in[0][:, None, :], w_exp_out[0],
                  b_exp_out[0][:, None, :])
    out = _combine(dst8, h1, rg, ln2_g, ln2_b, yb)
    return out[None]
```
